```python
import jax, jax.numpy as jnp
from jax import lax
import numpy as np

D_MODEL = 1024
BATCH = 4
SEQ = 8192
DEPTH = 1
DEC_BATCH = 128
DEC_SEQ = 4
PAST_LEN = 16384
PAGE_SIZE = 128

N_HEADS = 8
N_KV_HEADS = 2
HEAD_DIM = 64
GROUP = N_HEADS // N_KV_HEADS
ATTN_W = N_HEADS * HEAD_DIM
KV_W = N_KV_HEADS * HEAD_DIM
WINDOW = 128
LRU_W = D_MODEL // 2
LRU_BLOCKS = 8
LRU_BW = LRU_W // LRU_BLOCKS
CONV_W = 4
LRU_C = 8.0
MIX_W = ATTN_W + LRU_W
IN_W = ATTN_W + 2 * KV_W + 2 * LRU_W
Q_END = ATTN_W
K_END = Q_END + KV_W
V_END = K_END + KV_W
X_END = V_END + LRU_W
D_FF = -(-8 * D_MODEL // (3 * 256)) * 256
EPS = 1e-6
NEG = -1e30

kernel_name = "hymba_swa_sink_rglru_decoder_step"


def _rmsnorm(x, g):
    xf = x.astype(jnp.float32)
    y = xf * lax.rsqrt(jnp.mean(xf * xf, axis=-1, keepdims=True) + EPS) * g.astype(jnp.float32)
    return y.astype(x.dtype)


def _alibi_slopes():
    return jnp.exp2(-8.0 * jnp.arange(1, N_HEADS + 1, dtype=jnp.float32) / N_HEADS)


def _sink_attend(q, kb, vb, dist, sinks):
    s = jnp.einsum('bnqkgd,bnskd->bnkgqs', q, kb).astype(jnp.float32) * (HEAD_DIM ** -0.5)
    slopes = _alibi_slopes().reshape(N_KV_HEADS, GROUP, 1, 1)
    valid = ((dist >= 0) & (dist < WINDOW))[None, :, None, None]
    distf = dist.astype(jnp.float32)[None, :, None, None]
    s = jnp.where(valid, s - slopes * distf, NEG)
    sink = sinks.astype(jnp.float32).reshape(N_KV_HEADS, GROUP, 1, 1)
    m = jnp.maximum(jnp.max(s, axis=-1, keepdims=True), sink)
    p = jnp.exp(s - m)
    p = p / (jnp.sum(p, axis=-1, keepdims=True) + jnp.exp(sink - m))
    out = jnp.einsum('bnkgqs,bnskd->bnqkgd', p.astype(vb.dtype), vb)
    B, n, Tq = out.shape[:3]
    return out.reshape(B, n * Tq, ATTN_W)


def _attn_prompt(q, k, v, sinks):
    B, T = q.shape[:2]
    nb = T // WINDOW
    qb = q.reshape(B, nb, WINDOW, N_KV_HEADS, GROUP, HEAD_DIM)

    def band(t):
        tb = t.reshape(B, nb, WINDOW, N_KV_HEADS, HEAD_DIM)
        prev = jnp.pad(tb[:, :-1], ((0, 0), (1, 0), (0, 0), (0, 0), (0, 0)))
        return jnp.concatenate([prev, tb], axis=2)

    i = jnp.arange(WINDOW)[:, None]
    j = jnp.arange(2 * WINDOW)[None, :]
    d = WINDOW + i - j
    first = (jnp.arange(nb)[:, None, None] == 0) & (j[None] < WINDOW)
    dist = jnp.where(first, -1, d[None])
    out = _sink_attend(qb, band(k), band(v), dist, sinks)
    return out, k[:, -WINDOW:], v[:, -WINDOW:]


def _attn_sample(q, k, v, ck, cv, sinks):
    T = q.shape[1]
    kall = jnp.concatenate([ck.astype(k.dtype), k], axis=1)
    vall = jnp.concatenate([cv.astype(v.dtype), v], axis=1)
    i = jnp.arange(T)[:, None]
    j = jnp.arange(WINDOW + T)[None, :]
    dist = (WINDOW + i - j)[None]
    out = _sink_attend(q[:, None], kall[:, None], vall[:, None], dist, sinks)
    return out, kall[:, -WINDOW:], vall[:, -WINDOW:]


def _causal_conv(x, prefix, w, b):
    T = x.shape[1]
    xp = jnp.concatenate([prefix.astype(x.dtype), x], axis=1)
    out = b + w[0] * xp[:, 0:T]
    for kk in range(1, CONV_W):
        out = out + w[kk] * xp[:, kk:kk + T]
    return out, xp[:, T:]


def _block_diag(x, w, b):
    B, T = x.shape[:2]
    xr = x.reshape(B, T, LRU_BLOCKS, LRU_BW)
    return jnp.einsum('btnd,nde->btne', xr, w).reshape(B, T, LRU_W) + b


def _rg_lru(x, h0, w_a, b_a, w_i, b_i, lam):
    xf = x.astype(jnp.float32)
    r = jax.nn.sigmoid(_block_diag(x, w_a, b_a).astype(jnp.float32))
    ig = jax.nn.sigmoid(_block_diag(x, w_i, b_i).astype(jnp.float32))
    log_a = -LRU_C * r * jax.nn.softplus(-lam.astype(jnp.float32))
    a = jnp.exp(log_a)
    bterm = jnp.sqrt(-jnp.expm1(2.0 * log_a)) * (ig * xf)

    def step(h, ab):
        a_t, b_t = ab
        h = a_t * h + b_t
        return h, h

    h_last, hs = lax.scan(step, h0.astype(jnp.float32), (jnp.swapaxes(a, 0, 1), jnp.swapaxes(bterm, 0, 1)))
    return jnp.swapaxes(hs, 0, 1).astype(x.dtype), h_last


def _layer(x, mode, ck, cv, conv_prefix, h0, g_mix, w_in, b_in, attn_sinks, conv_w, conv_b,
           w_a, b_a, w_i, b_i, lam, w_out, b_out, g_ffn, w_ffn_in, w_ffn_out):
    B, T, _ = x.shape
    u = _rmsnorm(x, g_mix)
    z = u @ w_in + b_in
    q, k, v, xb, yb = jnp.split(z, [Q_END, K_END, V_END, X_END], axis=-1)
    q = q.reshape(B, T, N_KV_HEADS, GROUP, HEAD_DIM)
    k = k.reshape(B, T, N_KV_HEADS, HEAD_DIM)
    v = v.reshape(B, T, N_KV_HEADS, HEAD_DIM)
    if mode == 'prompt':
        attn, nk, nv = _attn_prompt(q, k, v, attn_sinks)
    else:
        attn, nk, nv = _attn_sample(q, k, v, ck, cv, attn_sinks)
    xc, nconv = _causal_conv(xb, conv_prefix, conv_w, conv_b)
    lru, nh = _rg_lru(xc, h0, w_a, b_a, w_i, b_i, lam)
    mix = jnp.concatenate([attn, lru * jax.nn.gelu(yb)], axis=-1)
    h = x + mix @ w_out + b_out
    gate, up = jnp.split(_rmsnorm(h, g_ffn) @ w_ffn_in, [D_FF], axis=-1)
    y = h + (jax.nn.silu(gate) * up) @ w_ffn_out
    return y, nk, nv, nconv, nh


def setup_inputs(seed: int = 0) -> dict:
    key = jax.random.key(seed)
    ks = jax.random.split(key, 23)

    def nrm(k, shape, scale):
        return jax.random.normal(k, shape, jnp.float32) * scale

    u = jax.random.uniform(ks[16], (DEPTH, LRU_W), jnp.float32, minval=0.9, maxval=0.999)
    return {
        'x_prompt': nrm(ks[0], (BATCH, SEQ, D_MODEL), 1.0),
        'x_sample': nrm(ks[1], (DEC_BATCH, DEC_SEQ, D_MODEL), 1.0),
        'cache_k': nrm(ks[2], (DEPTH, DEC_BATCH, WINDOW, N_KV_HEADS, HEAD_DIM), 1.0),
        'cache_v': nrm(ks[3], (DEPTH, DEC_BATCH, WINDOW, N_KV_HEADS, HEAD_DIM), 1.0),
        'state_conv': nrm(ks[4], (DEPTH, DEC_BATCH, CONV_W - 1, LRU_W), 1.0),
        'state_h': nrm(ks[5], (DEPTH, DEC_BATCH, LRU_W), 0.5),
        'g_mix': 1.0 + nrm(ks[6], (DEPTH, D_MODEL), 0.02),
        'w_in': nrm(ks[7], (DEPTH, D_MODEL, IN_W), D_MODEL ** -0.5),
        'b_in': nrm(ks[8], (DEPTH, IN_W), 0.02),
        'attn_sinks': nrm(ks[9], (DEPTH, N_HEADS), 0.5),
        'conv_w': nrm(ks[10], (DEPTH, CONV_W, LRU_W), CONV_W ** -0.5),
        'conv_b': nrm(ks[11], (DEPTH, LRU_W), 0.02),
        'w_a': nrm(ks[12], (DEPTH, LRU_BLOCKS, LRU_BW, LRU_BW), LRU_BW ** -0.5),
        'b_a': nrm(ks[13], (DEPTH, LRU_W), 0.02),
        'w_i': nrm(ks[14], (DEPTH, LRU_BLOCKS, LRU_BW, LRU_BW), LRU_BW ** -0.5),
        'b_i': nrm(ks[15], (DEPTH, LRU_W), 0.02),
        'lam': jnp.log(u) - jnp.log1p(-u),
        'w_out': nrm(ks[17], (DEPTH, MIX_W, D_MODEL), MIX_W ** -0.5),
        'b_out': nrm(ks[18], (DEPTH, D_MODEL), 0.02),
        'g_ffn': 1.0 + nrm(ks[19], (DEPTH, D_MODEL), 0.02),
        'w_ffn_in': nrm(ks[20], (DEPTH, D_MODEL, 2 * D_FF), D_MODEL ** -0.5),
        'w_ffn_out': nrm(ks[21], (DEPTH, D_FF, D_MODEL), D_FF ** -0.5),
        'g_final': 1.0 + nrm(ks[22], (D_MODEL,), 0.02),
    }


def reference(x_prompt, x_sample, cache_k, cache_v, state_conv, state_h, g_mix, w_in, b_in,
              attn_sinks, conv_w, conv_b, w_a, b_a, w_i, b_i, lam, w_out, b_out, g_ffn,
              w_ffn_in, w_ffn_out, g_final):
    yp, ys = x_prompt, x_sample
    pk, pv, pc, ph, sk, sv, sc, sh = [], [], [], [], [], [], [], []
    conv0 = jnp.zeros((x_prompt.shape[0], CONV_W - 1, LRU_W), x_prompt.dtype)
    h_zero = jnp.zeros((x_prompt.shape[0], LRU_W), jnp.float32)
    for l in range(DEPTH):
        lp = (g_mix[l], w_in[l], b_in[l], attn_sinks[l], conv_w[l], conv_b[l], w_a[l], b_a[l],
              w_i[l], b_i[l], lam[l], w_out[l], b_out[l], g_ffn[l], w_ffn_in[l], w_ffn_out[l])
        yp, k1, v1, c1, h1 = _layer(yp, 'prompt', None, None, conv0, h_zero, *lp)
        ys, k2, v2, c2, h2 = _layer(ys, 'sample', cache_k[l], cache_v[l], state_conv[l], state_h[l], *lp)
        pk.append(k1); pv.append(v1); pc.append(c1); ph.append(h1)
        sk.append(k2); sv.append(v2); sc.append(c2); sh.append(h2)
    y_prompt = _rmsnorm(yp, g_final)
    y_sample = _rmsnorm(ys, g_final)
    return (y_prompt, y_sample,
            jnp.stack(pk), jnp.stack(pv), jnp.stack(pc), jnp.stack(ph),
            jnp.stack(sk), jnp.stack(sv), jnp.stack(sc), jnp.stack(sh))
```

```python
import functools

import jax
import jax.numpy as jnp
from jax import lax
from jax.experimental import pallas as pl
from jax.experimental.pallas import tpu as pltpu

D_MODEL = 1024
N_HEADS = 8
N_KV_HEADS = 2
HEAD_DIM = 64
GROUP = N_HEADS // N_KV_HEADS
ATTN_W = N_HEADS * HEAD_DIM
KV_W = N_KV_HEADS * HEAD_DIM
WINDOW = 128
LRU_W = D_MODEL // 2
LRU_BLOCKS = 8
LRU_BW = LRU_W // LRU_BLOCKS
CONV_W = 4
LRU_C = 8.0
IN_W = ATTN_W + 2 * KV_W + 2 * LRU_W
Q_END = ATTN_W
K_END = Q_END + KV_W
V_END = K_END + KV_W
X_END = V_END + LRU_W
D_FF = 2816
EPS = 1e-6
NEG = -1e30

SUBLANES = 8
LANES = 128
VMEM_LIMIT_BYTES = 56 * 1024 * 1024

PROMPT_TILE = 512
FFN_CHUNK = 256
N_PAIRS = N_HEADS // 2
SAMPLE_SEQS = 32
PAIR_ROWS = N_HEADS * SUBLANES
SQ_W = N_HEADS * KV_W

F32 = jnp.float32
BF16 = jnp.bfloat16


def _dot(a, b):
  return jnp.dot(a, b, preferred_element_type=F32)


def _dot_t(a, b):
  return lax.dot_general(a, b, (((1,), (1,)), ((), ())), preferred_element_type=F32)


def _rmsnorm(x, g):
  return x * lax.rsqrt(jnp.mean(x * x, axis=-1, keepdims=True) + EPS) * g


def _alibi_slope(h):
  return 2.0 ** (-8.0 * (h + 1) / N_HEADS)


def _lru_coeffs(xc, wa_ref, ba_ref, wi_ref, bi_ref, lam_ref):
  xcb = xc.astype(BF16)
  half = LRU_W // 2
  lo, hi = xcb[:, :half], xcb[:, half:]
  r = jnp.concatenate([_dot(lo, wa_ref[0]), _dot(hi, wa_ref[1])], axis=-1) + ba_ref[...]
  ig = jnp.concatenate([_dot(lo, wi_ref[0]), _dot(hi, wi_ref[1])], axis=-1) + bi_ref[...]
  r = jax.nn.sigmoid(r)
  ig = jax.nn.sigmoid(ig)
  nlam = -lam_ref[...]
  softplus = jnp.maximum(nlam, 0.0) + jnp.log1p(jnp.exp(-jnp.abs(nlam)))
  log_a = -LRU_C * r * softplus
  a = jnp.exp(log_a)
  gain = jnp.sqrt(-jnp.tanh(log_a) * (a * a + 1.0))
  return a, gain * (ig * xc)


def _scan_within_groups(a, b, period):
  t, w = a.shape
  a3 = a.reshape(t // SUBLANES, SUBLANES, w)
  b3 = b.reshape(t // SUBLANES, SUBLANES, w)
  pos = lax.broadcasted_iota(jnp.int32, (1, SUBLANES, w), 1) % period
  d = 1
  while d < period:
    keep = pos >= d
    a_prev = jnp.where(keep, pltpu.roll(a3, d, axis=1), 1.0)
    b_prev = jnp.where(keep, pltpu.roll(b3, d, axis=1), 0.0)
    b3 = a3 * b_prev + b3
    a3 = a3 * a_prev
    d *= 2
  return a3.reshape(t, w), b3.reshape(t, w)


def _ffn_and_final_norm(hres, gffn_ref, wfi_ref, wfo_ref, gfin_ref):
  hn = _rmsnorm(hres, gffn_ref[...]).astype(BF16)
  acc = hres
  for c in range(D_FF // FFN_CHUNK):
    lo = c * FFN_CHUNK
    gate = _dot(hn, wfi_ref[:, lo:lo + FFN_CHUNK])
    up = _dot(hn, wfi_ref[:, D_FF + lo:D_FF + lo + FFN_CHUNK])
    act = (jax.nn.silu(gate) * up).astype(BF16)
    acc = acc + _dot(act, wfo_ref[lo:lo + FFN_CHUNK, :])
  return _rmsnorm(acc, gfin_ref[...])


def _prompt_kernel(
    sinks_ref, x_ref, gmix_ref, win_ref, bin_ref, convw_ref, convb_ref, wa_ref, ba_ref,
    wi_ref, bi_ref, lam_ref, wout_ref, bout_ref, gffn_ref, wfi_ref, wfo_ref, gfin_ref,
    y_ref, nk_ref, nv_ref, nconv_ref, nh_ref,
    q_s, kz_s, vz_s, xpad_s, a_s, b_s, hs_s, hcar_s, attn_s):
  tq = PROMPT_TILE
  t_idx = pl.program_id(1)
  first_tile = t_idx == 0
  last_tile = t_idx == pl.num_programs(1) - 1

  @pl.when(first_tile)
  def _():
    kz_s[:, 0:WINDOW, :] = jnp.zeros((4, WINDOW, KV_W), BF16)
    vz_s[:, 0:WINDOW, :] = jnp.zeros((4, WINDOW, KV_W), BF16)
    xpad_s[0:SUBLANES, :] = jnp.zeros((SUBLANES, LRU_W), F32)
    hcar_s[...] = jnp.zeros((SUBLANES, LRU_W), F32)

  x = x_ref[0]
  u = _rmsnorm(x, gmix_ref[...]).astype(BF16)

  q = _dot(u, win_ref[:, 0:Q_END]) + bin_ref[:, 0:Q_END]
  q_s[...] = (q * (HEAD_DIM ** -0.5)).astype(BF16)
  kv = _dot(u, win_ref[:, Q_END:V_END]) + bin_ref[:, Q_END:V_END]
  k = kv[:, 0:KV_W]
  v = kv[:, KV_W:2 * KV_W]
  xb = _dot(u, win_ref[:, V_END:X_END]) + bin_ref[:, V_END:X_END]
  yb = _dot(u, win_ref[:, X_END:IN_W]) + bin_ref[:, X_END:IN_W]

  lane = lax.broadcasted_iota(jnp.int32, (tq, KV_W), 1)
  is_lo = lane < HEAD_DIM
  for src, dst in ((k, kz_s), (v, vz_s)):
    swapped = pltpu.roll(src, HEAD_DIM, axis=1)
    dst[0, WINDOW:WINDOW + tq, :] = jnp.where(is_lo, src, 0.0).astype(BF16)
    dst[1, WINDOW:WINDOW + tq, :] = jnp.where(is_lo, 0.0, swapped).astype(BF16)
    dst[2, WINDOW:WINDOW + tq, :] = jnp.where(is_lo, swapped, 0.0).astype(BF16)
    dst[3, WINDOW:WINDOW + tq, :] = jnp.where(is_lo, 0.0, src).astype(BF16)

  row = lax.broadcasted_iota(jnp.int32, (WINDOW, 2 * WINDOW), 0)
  col = lax.broadcasted_iota(jnp.int32, (WINDOW, 2 * WINDOW), 1)
  dist = WINDOW + row - col
  distf = dist.astype(F32)
  own_ok = (col >= WINDOW) & (dist >= 0)
  prev_dist = jnp.where(col < WINDOW, dist, 2 * WINDOW)
  lane_q = lax.broadcasted_iota(jnp.int32, (WINDOW, KV_W), 1)

  def attn_block(j, carry):
    r0 = pl.multiple_of(j * WINDOW, WINDOW)
    no_prev = jnp.logical_and(first_tile, j == 0)
    valid = own_ok | (prev_dist < jnp.where(no_prev, 0, WINDOW))
    qb = q_s[pl.ds(r0, WINDOW), :]
    for p in range(N_PAIRS):
      g = p // (N_PAIRS // N_KV_HEADS)
      kst = jnp.concatenate(
          [kz_s[2 * g, pl.ds(r0, 2 * WINDOW), :], kz_s[2 * g + 1, pl.ds(r0, 2 * WINDOW), :]], axis=0)
      vst = jnp.concatenate(
          [vz_s[2 * g, pl.ds(r0, 2 * WINDOW), :], vz_s[2 * g + 1, pl.ds(r0, 2 * WINDOW), :]], axis=0)
      s = _dot_t(qb[:, p * KV_W:(p + 1) * KV_W], kst)
      probs = []
      inv = []
      for hh in range(2):
        h = 2 * p + hh
        sink = sinks_ref[h]
        sh = s[:, hh * 2 * WINDOW:(hh + 1) * 2 * WINDOW]
        sh = jnp.where(valid, sh - _alibi_slope(h) * distf, NEG)
        m = jnp.maximum(jnp.max(sh, axis=-1, keepdims=True), sink)
        e = jnp.exp(sh - m)
        denom = jnp.sum(e, axis=-1, keepdims=True) + jnp.exp(sink - m)
        probs.append(e.astype(BF16))
        inv.append(1.0 / denom)
      o = _dot(jnp.concatenate(probs, axis=-1), vst)
      o = o * jnp.where(lane_q < HEAD_DIM, inv[0], inv[1])
      attn_s[pl.ds(r0, WINDOW), p * KV_W:(p + 1) * KV_W] = o.astype(BF16)
    return carry

  lax.fori_loop(0, tq // WINDOW, attn_block, 0)

  kz_s[:, 0:WINDOW, :] = kz_s[:, tq:tq + WINDOW, :]
  vz_s[:, 0:WINDOW, :] = vz_s[:, tq:tq + WINDOW, :]

  xpad_s[SUBLANES:SUBLANES + tq, :] = xb
  xc = convb_ref[...] + convw_ref[CONV_W - 1:CONV_W, :] * xb
  for kk in range(CONV_W - 1):
    off = SUBLANES - (CONV_W - 1) + kk
    xc = xc + convw_ref[kk:kk + 1, :] * xpad_s[off:off + tq, :]
  xpad_s[0:SUBLANES, :] = xpad_s[tq:tq + SUBLANES, :]

  a, b = _lru_coeffs(xc, wa_ref, ba_ref, wi_ref, bi_ref, lam_ref)
  a, b = _scan_within_groups(a, b, SUBLANES)
  a_s[...] = a
  b_s[...] = b

  def chain(gi, h):
    r0 = pl.multiple_of(gi * SUBLANES, SUBLANES)
    hg = a_s[pl.ds(r0, SUBLANES), :] * h + b_s[pl.ds(r0, SUBLANES), :]
    hs_s[pl.ds(r0, SUBLANES), :] = hg
    return jnp.broadcast_to(hg[SUBLANES - 1:SUBLANES, :], (SUBLANES, LRU_W))

  h_last = lax.fori_loop(0, tq // SUBLANES, chain, hcar_s[...], unroll=8)
  hcar_s[...] = h_last
  lru = hs_s[...] * jax.nn.gelu(yb)

  @pl.when(last_tile)
  def _():
    nk_ref[0] = k[tq - WINDOW:tq, :]
    nv_ref[0] = v[tq - WINDOW:tq, :]
    nconv_ref[0] = xb[tq - (CONV_W - 1):tq, :]
    nh_ref[0] = h_last[0:1, :]

  hres = (x + _dot(attn_s[...], wout_ref[0:ATTN_W, :])
          + _dot(lru.astype(BF16), wout_ref[ATTN_W:ATTN_W + LRU_W, :]) + bout_ref[...])
  y_ref[0] = _ffn_and_final_norm(hres, gffn_ref, wfi_ref, wfo_ref, gfin_ref)


def _whole(space):
  return pl.BlockSpec(memory_space=space)


def _prompt_call(x, sinks, weights):
  batch, seq, _ = x.shape
  tq = PROMPT_TILE
  n_t = seq // tq
  vm = _whole(pltpu.VMEM)
  in_specs = [_whole(pltpu.SMEM), pl.BlockSpec((1, tq, D_MODEL), lambda b, t: (b, t, 0))]
  in_specs += [vm] * len(weights)
  out_shape = (
      jax.ShapeDtypeStruct((batch, seq, D_MODEL), F32),
      jax.ShapeDtypeStruct((batch, WINDOW, KV_W), F32),
      jax.ShapeDtypeStruct((batch, WINDOW, KV_W), F32),
      jax.ShapeDtypeStruct((batch, CONV_W - 1, LRU_W), F32),
      jax.ShapeDtypeStruct((batch, 1, LRU_W), F32),
  )
  out_specs = (
      pl.BlockSpec((1, tq, D_MODEL), lambda b, t: (b, t, 0)),
      pl.BlockSpec((1, WINDOW, KV_W), lambda b, t: (b, 0, 0)),
      pl.BlockSpec((1, WINDOW, KV_W), lambda b, t: (b, 0, 0)),
      pl.BlockSpec((1, CONV_W - 1, LRU_W), lambda b, t: (b, 0, 0)),
      pl.BlockSpec((1, 1, LRU_W), lambda b, t: (b, 0, 0)),
  )
  scratch = [
      pltpu.VMEM((tq, ATTN_W), BF16),
      pltpu.VMEM((4, WINDOW + tq, KV_W), BF16),
      pltpu.VMEM((4, WINDOW + tq, KV_W), BF16),
      pltpu.VMEM((SUBLANES + tq, LRU_W), F32),
      pltpu.VMEM((tq, LRU_W), F32),
      pltpu.VMEM((tq, LRU_W), F32),
      pltpu.VMEM((tq, LRU_W), F32),
      pltpu.VMEM((SUBLANES, LRU_W), F32),
      pltpu.VMEM((tq, ATTN_W), BF16),
  ]
  return pl.pallas_call(
      _prompt_kernel,
      grid=(batch, n_t),
      in_specs=in_specs,
      out_specs=out_specs,
      out_shape=out_shape,
      scratch_shapes=scratch,
      compiler_params=pltpu.CompilerParams(
          dimension_semantics=("arbitrary", "arbitrary"),
          vmem_limit_bytes=VMEM_LIMIT_BYTES),
      name="prompt_layer",
  )(sinks, x, *weights)


def _sample_kernel(
    sink_ref, x_ref, ck_ref, cv_ref, cpad_ref, h0_ref, gmix_ref, win_ref, bin_ref, convw_ref,
    convb_ref, wa_ref, ba_ref, wi_ref, bi_ref, lam_ref, wout_ref, bout_ref, gffn_ref, wfi_ref,
    wfo_ref, gfin_ref,
    y_ref, knew_ref, vnew_ref, xb_ref, hs_ref,
    zq_s, attn_s):
  rows = SAMPLE_SEQS * CONV_W
  n_tok = CONV_W
  x = x_ref[...]
  u = _rmsnorm(x, gmix_ref[...]).astype(BF16)

  zq_s[...] = (_dot(u, win_ref[:, 0:SQ_W]) + bin_ref[:, 0:SQ_W]) * (HEAD_DIM ** -0.5)
  kv = _dot(u, win_ref[:, SQ_W:SQ_W + 2 * KV_W]) + bin_ref[:, SQ_W:SQ_W + 2 * KV_W]
  knew_ref[...] = kv[:, 0:KV_W]
  vnew_ref[...] = kv[:, KV_W:2 * KV_W]
  xs = SQ_W + 2 * KV_W
  xb = _dot(u, win_ref[:, xs:xs + LRU_W]) + bin_ref[:, xs:xs + LRU_W]
  yb = _dot(u, win_ref[:, xs + LRU_W:xs + 2 * LRU_W]) + bin_ref[:, xs + LRU_W:xs + 2 * LRU_W]
  xb_ref[...] = xb

  prow = lax.broadcasted_iota(jnp.int32, (PAIR_ROWS, 2 * WINDOW), 0)
  pcol = lax.broadcasted_iota(jnp.int32, (PAIR_ROWS, 2 * WINDOW), 1)
  seq_r = (prow % SUBLANES) // n_tok
  tok_r = prow % n_tok
  dist_c = WINDOW + tok_r - (pcol % WINDOW)
  valid_c = (seq_r == pcol // WINDOW) & (dist_c >= 0) & (dist_c < WINDOW)
  nrow = lax.broadcasted_iota(jnp.int32, (PAIR_ROWS, LANES), 0)
  ncol = lax.broadcasted_iota(jnp.int32, (PAIR_ROWS, LANES), 1)
  dist_n = (nrow % n_tok) - (ncol % n_tok)
  valid_n = (ncol < SUBLANES) & ((nrow % SUBLANES) // n_tok == ncol // n_tok) & (dist_n >= 0)
  slope_c = jnp.zeros((PAIR_ROWS, 2 * WINDOW), F32)
  slope_n = jnp.zeros((PAIR_ROWS, LANES), F32)
  for h in range(N_HEADS):
    slope_c = jnp.where(prow // SUBLANES == h, _alibi_slope(h), slope_c)
    slope_n = jnp.where(nrow // SUBLANES == h, _alibi_slope(h), slope_n)
  bias_c = slope_c * dist_c.astype(F32)
  bias_n = slope_n * dist_n.astype(F32)
  sink = sink_ref[...]
  pad = jnp.zeros((LANES - SUBLANES, KV_W), F32)

  def pair(c, carry):
    r0 = pl.multiple_of(c * SUBLANES, SUBLANES)
    lhs = jnp.concatenate(
        [zq_s[pl.ds(r0, SUBLANES), h * KV_W:(h + 1) * KV_W] for h in range(N_HEADS)],
        axis=0).astype(BF16)
    kc = ck_ref[pl.ds(2 * c, 2)].reshape(2 * WINDOW, KV_W).astype(BF16)
    vc = cv_ref[pl.ds(2 * c, 2)].reshape(2 * WINDOW, KV_W).astype(BF16)
    kn = jnp.concatenate([knew_ref[pl.ds(r0, SUBLANES), :], pad], axis=0).astype(BF16)
    vn = jnp.concatenate([vnew_ref[pl.ds(r0, SUBLANES), :], pad], axis=0).astype(BF16)
    s_c = jnp.where(valid_c, _dot_t(lhs, kc) - bias_c, NEG)
    s_n = jnp.where(valid_n, _dot_t(lhs, kn) - bias_n, NEG)
    m = jnp.maximum(
        jnp.maximum(jnp.max(s_c, axis=-1, keepdims=True), jnp.max(s_n, axis=-1, keepdims=True)),
        sink)
    e_c = jnp.exp(s_c - m)
    e_n = jnp.exp(s_n - m)
    denom = (jnp.sum(e_c, axis=-1, keepdims=True) + jnp.sum(e_n, axis=-1, keepdims=True)
             + jnp.exp(sink - m))
    o = (_dot(e_c.astype(BF16), vc) + _dot(e_n.astype(BF16), vn)) * (1.0 / denom)
    for h in range(N_HEADS):
      attn_s[pl.ds(r0, SUBLANES), h * KV_W:(h + 1) * KV_W] = o[h * SUBLANES:(h + 1) * SUBLANES, :]
    return carry

  lax.fori_loop(0, SAMPLE_SEQS // 2, pair, 0)

  tok = lax.broadcasted_iota(jnp.int32, (rows, LRU_W), 0) % n_tok
  cpad = cpad_ref[...]
  xc = convb_ref[...] + convw_ref[CONV_W - 1:CONV_W, :] * xb
  for shift in range(1, CONV_W):
    saved = cpad if shift == CONV_W - 1 else pltpu.roll(cpad, rows - (CONV_W - 1 - shift), axis=0)
    prev = jnp.where(tok >= shift, pltpu.roll(xb, shift, axis=0), saved)
    xc = xc + convw_ref[CONV_W - 1 - shift:CONV_W - shift, :] * prev

  a, b = _lru_coeffs(xc, wa_ref, ba_ref, wi_ref, bi_ref, lam_ref)
  a, b = _scan_within_groups(a, b, n_tok)
  hs = a * h0_ref[...] + b
  hs_ref[...] = hs
  lru = hs * jax.nn.gelu(yb)

  hres = (x + _dot(attn_s[...].astype(BF16), wout_ref[0:SQ_W, :])
          + _dot(lru.astype(BF16), wout_ref[SQ_W:SQ_W + LRU_W, :]) + bout_ref[...])
  y_ref[...] = _ffn_and_final_norm(hres, gffn_ref, wfi_ref, wfo_ref, gfin_ref)


def _sample_call(x, ck, cv, cpad, h0rep, sink_rows, weights):
  n_rows = x.shape[0]
  rows = SAMPLE_SEQS * CONV_W
  n_steps = n_rows // rows
  vm = _whole(pltpu.VMEM)
  row_spec = lambda w: pl.BlockSpec((rows, w), lambda i: (i, 0))
  cache_spec = pl.BlockSpec((SAMPLE_SEQS, WINDOW, KV_W), lambda i: (i, 0, 0))
  in_specs = [vm, row_spec(D_MODEL), cache_spec, cache_spec, row_spec(LRU_W), row_spec(LRU_W)]
  in_specs += [vm] * len(weights)
  out_shape = (
      jax.ShapeDtypeStruct((n_rows, D_MODEL), F32),
      jax.ShapeDtypeStruct((n_rows, KV_W), F32),
      jax.ShapeDtypeStruct((n_rows, KV_W), F32),
      jax.ShapeDtypeStruct((n_rows, LRU_W), F32),
      jax.ShapeDtypeStruct((n_rows, LRU_W), F32),
  )
  out_specs = (row_spec(D_MODEL), row_spec(KV_W), row_spec(KV_W), row_spec(LRU_W), row_spec(LRU_W))
  scratch = [
      pltpu.VMEM((rows, SQ_W), F32),
      pltpu.VMEM((rows, SQ_W), F32),
  ]
  return pl.pallas_call(
      _sample_kernel,
      grid=(n_steps,),
      in_specs=in_specs,
      out_specs=out_specs,
      out_shape=out_shape,
      scratch_shapes=scratch,
      compiler_params=pltpu.CompilerParams(
          dimension_semantics=("arbitrary",),
          vmem_limit_bytes=VMEM_LIMIT_BYTES),
      name="sample_layer",
  )(sink_rows, x, ck, cv, cpad, h0rep, *weights)


def _block_diag_halves(w):
  per = LRU_BLOCKS // 2
  out = jnp.zeros((2, per * LRU_BW, per * LRU_BW), w.dtype)
  for i in range(LRU_BLOCKS):
    j = i % per
    out = out.at[i // per, j * LRU_BW:(j + 1) * LRU_BW, j * LRU_BW:(j + 1) * LRU_BW].set(w[i])
  return out.astype(BF16)


def _head_to_kv_half(cols):
  parts = []
  zero = jnp.zeros(cols.shape[:-1] + (HEAD_DIM,), cols.dtype)
  for h in range(N_HEADS):
    blk = cols[..., h * HEAD_DIM:(h + 1) * HEAD_DIM]
    parts += [blk, zero] if h // GROUP == 0 else [zero, blk]
  return jnp.concatenate(parts, axis=-1)


def kernel(x_prompt, x_sample, cache_k, cache_v, state_conv, state_h, g_mix, w_in, b_in,
           attn_sinks, conv_w, conv_b, w_a, b_a, w_i, b_i, lam, w_out, b_out, g_ffn,
           w_ffn_in, w_ffn_out, g_final):
  depth = g_mix.shape[0]
  assert depth == 1
  batch, seq, _ = x_prompt.shape
  dec_batch, dec_seq, _ = x_sample.shape
  assert seq % PROMPT_TILE == 0 and dec_seq == CONV_W and dec_batch % SAMPLE_SEQS == 0

  row = lambda p: p.reshape(1, -1)
  common = dict(
      convw=conv_w[0], convb=row(conv_b[0]), wa=_block_diag_halves(w_a[0]), ba=row(b_a[0]),
      wi=_block_diag_halves(w_i[0]), bi=row(b_i[0]), lam=row(lam[0]), bout=row(b_out[0]),
      gffn=row(g_ffn[0]), wfi=w_ffn_in[0].astype(BF16), wfo=w_ffn_out[0].astype(BF16),
      gfin=row(g_final))

  def weight_list(win, bin_, wout):
    c = common
    return [row(g_mix[0]), win, bin_, c["convw"], c["convb"], c["wa"], c["ba"], c["wi"], c["bi"],
            c["lam"], wout, c["bout"], c["gffn"], c["wfi"], c["wfo"], c["gfin"]]

  p_weights = weight_list(w_in[0].astype(BF16), row(b_in[0]), w_out[0].astype(BF16))
  y_p, nk_p, nv_p, nconv_p, nh_p = _prompt_call(x_prompt, attn_sinks[0], p_weights)

  win_s = jnp.concatenate([_head_to_kv_half(w_in[0][:, :Q_END]), w_in[0][:, Q_END:]], axis=-1)
  bin_s = jnp.concatenate([_head_to_kv_half(b_in[0][:Q_END]), b_in[0][Q_END:]], axis=-1)
  wout_s = jnp.concatenate([_head_to_kv_half(w_out[0][:ATTN_W].T).T, w_out[0][ATTN_W:]], axis=0)
  s_weights = weight_list(win_s.astype(BF16), row(bin_s), wout_s.astype(BF16))
  n_rows = dec_batch * dec_seq
  cpad = jnp.pad(state_conv[0], ((0, 0), (0, 1), (0, 0))).reshape(n_rows, LRU_W)
  h0rep = jnp.repeat(state_h[0], dec_seq, axis=0)
  sink_rows = jnp.repeat(attn_sinks[0], SUBLANES).reshape(PAIR_ROWS, 1)
  y_s, knew, vnew, xb_s, hs_s = _sample_call(
      x_sample.reshape(n_rows, D_MODEL), cache_k[0].reshape(dec_batch, WINDOW, KV_W),
      cache_v[0].reshape(dec_batch, WINDOW, KV_W), cpad, h0rep, sink_rows, s_weights)

  kv_shape = (dec_batch, dec_seq, N_KV_HEADS, HEAD_DIM)
  nk_s = jnp.concatenate([cache_k[0][:, dec_seq:], knew.reshape(kv_shape)], axis=1)
  nv_s = jnp.concatenate([cache_v[0][:, dec_seq:], vnew.reshape(kv_shape)], axis=1)
  nconv_s = xb_s.reshape(dec_batch, dec_seq, LRU_W)[:, dec_seq - (CONV_W - 1):]
  nh_s = hs_s.reshape(dec_batch, dec_seq, LRU_W)[:, dec_seq - 1]

  pkv = (1, batch, WINDOW, N_KV_HEADS, HEAD_DIM)
  return (y_p, y_s.reshape(dec_batch, dec_seq, D_MODEL),
          nk_p.reshape(pkv), nv_p.reshape(pkv), nconv_p[None], nh_p.reshape(1, batch, LRU_W),
          nk_s[None], nv_s[None], nconv_s[None], nh_s[None])
```

```python
import functools

import jax
import jax.numpy as jnp
from jax import lax
from jax.experimental import pallas as pl
from jax.experimental.pallas import tpu as pltpu

D_MODEL = 1024
N_HEADS = 8
N_KV_HEADS = 2
HEAD_DIM = 64
GROUP = N_HEADS // N_KV_HEADS
ATTN_W = N_HEADS * HEAD_DIM
KV_W = N_KV_HEADS * HEAD_DIM
WINDOW = 128
LRU_W = D_MODEL // 2
LRU_BLOCKS = 8
LRU_BW = LRU_W // LRU_BLOCKS
CONV_W = 4
LRU_C = 8.0
IN_W = ATTN_W + 2 * KV_W + 2 * LRU_W
Q_END = ATTN_W
K_END = Q_END + KV_W
V_END = K_END + KV_W
X_END = V_END + LRU_W
D_FF = 2816
EPS = 1e-6
NEG = -1e30

SUBLANES = 8
LANES = 128
VMEM_LIMIT_BYTES = 56 * 1024 * 1024

PROMPT_TILE = 512
FFN_CHUNK = 256
N_PAIRS = N_HEADS // 2
SAMPLE_SEQS = 32
PAIR_ROWS = N_HEADS * SUBLANES
SQ_W = N_HEADS * KV_W

F32 = jnp.float32
BF16 = jnp.bfloat16


def _dot(a, b):
  return jnp.dot(a, b, preferred_element_type=F32)


def _dot_t(a, b):
  return lax.dot_general(a, b, (((1,), (1,)), ((), ())), preferred_element_type=F32)


def _rmsnorm(x, g):
  return x * lax.rsqrt(jnp.mean(x * x, axis=-1, keepdims=True) + EPS) * g


def _alibi_slope(h):
  return 2.0 ** (-8.0 * (h + 1) / N_HEADS)


def _lru_coeffs(xc, wa_ref, ba_ref, wi_ref, bi_ref, lam_ref):
  xcb = xc.astype(BF16)
  half = LRU_W // 2
  lo, hi = xcb[:, :half], xcb[:, half:]
  r = jnp.concatenate([_dot(lo, wa_ref[0]), _dot(hi, wa_ref[1])], axis=-1) + ba_ref[...]
  ig = jnp.concatenate([_dot(lo, wi_ref[0]), _dot(hi, wi_ref[1])], axis=-1) + bi_ref[...]
  r = jax.nn.sigmoid(r)
  ig = jax.nn.sigmoid(ig)
  nlam = -lam_ref[...]
  softplus = jnp.maximum(nlam, 0.0) + jnp.log1p(jnp.exp(-jnp.abs(nlam)))
  log_a = -LRU_C * r * softplus
  a = jnp.exp(log_a)
  gain = jnp.sqrt(-jnp.tanh(log_a) * (a * a + 1.0))
  return a, gain * (ig * xc)


def _scan_within_groups(a, b, period):
  t, w = a.shape
  a3 = a.reshape(t // SUBLANES, SUBLANES, w)
  b3 = b.reshape(t // SUBLANES, SUBLANES, w)
  pos = lax.broadcasted_iota(jnp.int32, (1, SUBLANES, w), 1) % period
  d = 1
  while d < period:
    keep = pos >= d
    a_prev = jnp.where(keep, pltpu.roll(a3, d, axis=1), 1.0)
    b_prev = jnp.where(keep, pltpu.roll(b3, d, axis=1), 0.0)
    b3 = a3 * b_prev + b3
    a3 = a3 * a_prev
    d *= 2
  return a3.reshape(t, w), b3.reshape(t, w)


class _Ffn:
  def __init__(self, hres, gffn_ref, wfi_ref, wfo_ref, gfin_ref):
    self.hn = _rmsnorm(hres, gffn_ref[...]).astype(BF16)
    self.acc = hres
    self.wfi_ref, self.wfo_ref, self.gfin_ref = wfi_ref, wfo_ref, gfin_ref
    self.done = 0

  def run(self, n_chunks):
    for _ in range(n_chunks):
      lo = self.done * FFN_CHUNK
      gate = _dot(self.hn, self.wfi_ref[:, lo:lo + FFN_CHUNK])
      up = _dot(self.hn, self.wfi_ref[:, D_FF + lo:D_FF + lo + FFN_CHUNK])
      act = (jax.nn.silu(gate) * up).astype(BF16)
      self.acc = self.acc + _dot(act, self.wfo_ref[lo:lo + FFN_CHUNK, :])
      self.done += 1

  def finish(self):
    self.run(D_FF // FFN_CHUNK - self.done)
    return _rmsnorm(self.acc, self.gfin_ref[...])


def _prompt_kernel(
    sinks_ref, x_ref, gmix_ref, win_ref, bin_ref, convw_ref, convb_ref, wa_ref, ba_ref,
    wi_ref, bi_ref, lam_ref, wout_ref, bout_ref, gffn_ref, wfi_ref, wfo_ref, gfin_ref,
    y_ref, nk_ref, nv_ref, nconv_ref, nh_ref,
    q_s, kz_s, vz_s, xpad_s, hs_s, hcar_s, attn_s, hres_s, *, tiles_per_seq):
  tq = PROMPT_TILE
  step = pl.program_id(0)
  slot = step % 2
  t_idx = step % tiles_per_seq
  first_tile = t_idx == 0
  last_tile = t_idx == tiles_per_seq - 1

  @pl.when(step == 0)
  def _():
    hres_s[...] = jnp.zeros((2, tq, D_MODEL), F32)

  @pl.when(first_tile)
  def _():
    kz_s[:, 0:WINDOW, :] = jnp.zeros((4, WINDOW, KV_W), BF16)
    vz_s[:, 0:WINDOW, :] = jnp.zeros((4, WINDOW, KV_W), BF16)
    xpad_s[0:SUBLANES, :] = jnp.zeros((SUBLANES, LRU_W), F32)
    hcar_s[...] = jnp.zeros((SUBLANES, LRU_W), F32)

  ffn = _Ffn(hres_s[1 - slot], gffn_ref, wfi_ref, wfo_ref, gfin_ref)

  x = x_ref[0]
  u = _rmsnorm(x, gmix_ref[...]).astype(BF16)

  q = _dot(u, win_ref[:, 0:Q_END]) + bin_ref[:, 0:Q_END]
  q_s[...] = (q * (HEAD_DIM ** -0.5)).astype(BF16)
  kv = _dot(u, win_ref[:, Q_END:V_END]) + bin_ref[:, Q_END:V_END]
  k = kv[:, 0:KV_W]
  v = kv[:, KV_W:2 * KV_W]
  xb = _dot(u, win_ref[:, V_END:X_END]) + bin_ref[:, V_END:X_END]
  yb = _dot(u, win_ref[:, X_END:IN_W]) + bin_ref[:, X_END:IN_W]

  lane = lax.broadcasted_iota(jnp.int32, (tq, KV_W), 1)
  is_lo = lane < HEAD_DIM
  for src, dst in ((k, kz_s), (v, vz_s)):
    swapped = pltpu.roll(src, HEAD_DIM, axis=1)
    dst[0, WINDOW:WINDOW + tq, :] = jnp.where(is_lo, src, 0.0).astype(BF16)
    dst[1, WINDOW:WINDOW + tq, :] = jnp.where(is_lo, 0.0, swapped).astype(BF16)
    dst[2, WINDOW:WINDOW + tq, :] = jnp.where(is_lo, swapped, 0.0).astype(BF16)
    dst[3, WINDOW:WINDOW + tq, :] = jnp.where(is_lo, 0.0, src).astype(BF16)

  row = lax.broadcasted_iota(jnp.int32, (WINDOW, 2 * WINDOW), 0)
  col = lax.broadcasted_iota(jnp.int32, (WINDOW, 2 * WINDOW), 1)
  dist = WINDOW + row - col
  distf = dist.astype(F32)
  own_ok = (col >= WINDOW) & (dist >= 0)
  prev_dist = jnp.where(col < WINDOW, dist, 2 * WINDOW)
  lane_q = lax.broadcasted_iota(jnp.int32, (WINDOW, KV_W), 1)

  def attn_block(j):
    r0 = j * WINDOW
    prev_limit = jnp.where(first_tile, 0, WINDOW) if j == 0 else WINDOW
    valid = own_ok | (prev_dist < prev_limit)
    qb = q_s[r0:r0 + WINDOW, :]
    for p in range(N_PAIRS):
      g = p // (N_PAIRS // N_KV_HEADS)
      kst = jnp.concatenate(
          [kz_s[2 * g, r0:r0 + 2 * WINDOW, :], kz_s[2 * g + 1, r0:r0 + 2 * WINDOW, :]], axis=0)
      vst = jnp.concatenate(
          [vz_s[2 * g, r0:r0 + 2 * WINDOW, :], vz_s[2 * g + 1, r0:r0 + 2 * WINDOW, :]], axis=0)
      s = _dot_t(qb[:, p * KV_W:(p + 1) * KV_W], kst)
      probs = []
      inv = []
      for hh in range(2):
        h = 2 * p + hh
        sink = sinks_ref[h]
        sh = s[:, hh * 2 * WINDOW:(hh + 1) * 2 * WINDOW]
        sh = jnp.where(valid, sh - _alibi_slope(h) * distf, NEG)
        m = jnp.maximum(jnp.max(sh, axis=-1, keepdims=True), sink)
        e = jnp.exp(sh - m)
        denom = jnp.sum(e, axis=-1, keepdims=True) + jnp.exp(sink - m)
        probs.append(e.astype(BF16))
        inv.append(1.0 / denom)
      o = _dot(jnp.concatenate(probs, axis=-1), vst)
      o = o * jnp.where(lane_q < HEAD_DIM, inv[0], inv[1])
      attn_s[r0:r0 + WINDOW, p * KV_W:(p + 1) * KV_W] = o.astype(BF16)

  for j in range(tq // WINDOW):
    attn_block(j)
    ffn.run(1)

  kz_s[:, 0:WINDOW, :] = kz_s[:, tq:tq + WINDOW, :]
  vz_s[:, 0:WINDOW, :] = vz_s[:, tq:tq + WINDOW, :]

  xpad_s[SUBLANES:SUBLANES + tq, :] = xb
  xc = convb_ref[...] + convw_ref[CONV_W - 1:CONV_W, :] * xb
  for kk in range(CONV_W - 1):
    off = SUBLANES - (CONV_W - 1) + kk
    xc = xc + convw_ref[kk:kk + 1, :] * xpad_s[off:off + tq, :]
  xpad_s[0:SUBLANES, :] = xpad_s[tq:tq + SUBLANES, :]
  ffn.run(1)

  a, b = _lru_coeffs(xc, wa_ref, ba_ref, wi_ref, bi_ref, lam_ref)
  ffn.run(2)
  a, b = _scan_within_groups(a, b, SUBLANES)
  ffn.run(2)
  h = hcar_s[...]
  for gi in range(tq // SUBLANES):
    r0 = gi * SUBLANES
    hg = a[r0:r0 + SUBLANES, :] * h + b[r0:r0 + SUBLANES, :]
    hs_s[r0:r0 + SUBLANES, :] = hg
    h = jnp.broadcast_to(hg[SUBLANES - 1:SUBLANES, :], (SUBLANES, LRU_W))
  hcar_s[...] = h
  ffn.run(1)
  lru = hs_s[...] * jax.nn.gelu(yb)

  @pl.when(last_tile)
  def _():
    nk_ref[0] = k[tq - WINDOW:tq, :]
    nv_ref[0] = v[tq - WINDOW:tq, :]
    nconv_ref[0] = xb[tq - (CONV_W - 1):tq, :]
    nh_ref[0] = h[0:1, :]

  hres_s[slot] = (x + _dot(attn_s[...], wout_ref[0:ATTN_W, :])
                  + _dot(lru.astype(BF16), wout_ref[ATTN_W:ATTN_W + LRU_W, :]) + bout_ref[...])
  y_ref[0] = ffn.finish()


def _whole(space):
  return pl.BlockSpec(memory_space=space)


def _prompt_call(x, sinks, weights):
  batch, seq, _ = x.shape
  tq = PROMPT_TILE
  n_t = seq // tq
  n_tiles = batch * n_t
  vm = _whole(pltpu.VMEM)

  def mixer_tile(s):
    return jnp.minimum(s, n_tiles - 1)

  def ffn_tile(s):
    return jnp.maximum(s - 1, 0)

  in_specs = [_whole(pltpu.SMEM),
              pl.BlockSpec((1, tq, D_MODEL), lambda s: (mixer_tile(s) // n_t, mixer_tile(s) % n_t, 0))]
  in_specs += [vm] * len(weights)
  out_shape = (
      jax.ShapeDtypeStruct((batch, seq, D_MODEL), F32),
      jax.ShapeDtypeStruct((batch, WINDOW, KV_W), F32),
      jax.ShapeDtypeStruct((batch, WINDOW, KV_W), F32),
      jax.ShapeDtypeStruct((batch, CONV_W - 1, LRU_W), F32),
      jax.ShapeDtypeStruct((batch, 1, LRU_W), F32),
  )
  state_map = lambda s: (mixer_tile(s) // n_t, 0, 0)
  out_specs = (
      pl.BlockSpec((1, tq, D_MODEL), lambda s: (ffn_tile(s) // n_t, ffn_tile(s) % n_t, 0)),
      pl.BlockSpec((1, WINDOW, KV_W), state_map),
      pl.BlockSpec((1, WINDOW, KV_W), state_map),
      pl.BlockSpec((1, CONV_W - 1, LRU_W), state_map),
      pl.BlockSpec((1, 1, LRU_W), state_map),
  )
  scratch = [
      pltpu.VMEM((tq, ATTN_W), BF16),
      pltpu.VMEM((4, WINDOW + tq, KV_W), BF16),
      pltpu.VMEM((4, WINDOW + tq, KV_W), BF16),
      pltpu.VMEM((SUBLANES + tq, LRU_W), F32),
      pltpu.VMEM((tq, LRU_W), F32),
      pltpu.VMEM((SUBLANES, LRU_W), F32),
      pltpu.VMEM((tq, ATTN_W), BF16),
      pltpu.VMEM((2, tq, D_MODEL), F32),
  ]
  return pl.pallas_call(
      functools.partial(_prompt_kernel, tiles_per_seq=n_t),
      grid=(n_tiles + 1,),
      in_specs=in_specs,
      out_specs=out_specs,
      out_shape=out_shape,
      scratch_shapes=scratch,
      compiler_params=pltpu.CompilerParams(
          dimension_semantics=("arbitrary",),
          vmem_limit_bytes=VMEM_LIMIT_BYTES),
      name="prompt_layer",
  )(sinks, x, *weights)


def _sample_kernel(
    sink_ref, x_ref, ck_ref, cv_ref, cpad_ref, h0_ref, gmix_ref, win_ref, bin_ref, convw_ref,
    convb_ref, wa_ref, ba_ref, wi_ref, bi_ref, lam_ref, wout_ref, bout_ref, gffn_ref, wfi_ref,
    wfo_ref, gfin_ref,
    y_ref, knew_ref, vnew_ref, xb_ref, hs_ref,
    zq_s, attn_s):
  rows = SAMPLE_SEQS * CONV_W
  n_tok = CONV_W
  x = x_ref[...]
  u = _rmsnorm(x, gmix_ref[...]).astype(BF16)

  zq_s[...] = (_dot(u, win_ref[:, 0:SQ_W]) + bin_ref[:, 0:SQ_W]) * (HEAD_DIM ** -0.5)
  kv = _dot(u, win_ref[:, SQ_W:SQ_W + 2 * KV_W]) + bin_ref[:, SQ_W:SQ_W + 2 * KV_W]
  knew_ref[...] = kv[:, 0:KV_W]
  vnew_ref[...] = kv[:, KV_W:2 * KV_W]
  xs = SQ_W + 2 * KV_W
  xb = _dot(u, win_ref[:, xs:xs + LRU_W]) + bin_ref[:, xs:xs + LRU_W]
  yb = _dot(u, win_ref[:, xs + LRU_W:xs + 2 * LRU_W]) + bin_ref[:, xs + LRU_W:xs + 2 * LRU_W]
  xb_ref[...] = xb

  prow = lax.broadcasted_iota(jnp.int32, (PAIR_ROWS, 2 * WINDOW), 0)
  pcol = lax.broadcasted_iota(jnp.int32, (PAIR_ROWS, 2 * WINDOW), 1)
  seq_r = (prow % SUBLANES) // n_tok
  tok_r = prow % n_tok
  dist_c = WINDOW + tok_r - (pcol % WINDOW)
  valid_c = (seq_r == pcol // WINDOW) & (dist_c >= 0) & (dist_c < WINDOW)
  nrow = lax.broadcasted_iota(jnp.int32, (PAIR_ROWS, LANES), 0)
  ncol = lax.broadcasted_iota(jnp.int32, (PAIR_ROWS, LANES), 1)
  dist_n = (nrow % n_tok) - (ncol % n_tok)
  valid_n = (ncol < SUBLANES) & ((nrow % SUBLANES) // n_tok == ncol // n_tok) & (dist_n >= 0)
  slope_c = jnp.zeros((PAIR_ROWS, 2 * WINDOW), F32)
  slope_n = jnp.zeros((PAIR_ROWS, LANES), F32)
  for h in range(N_HEADS):
    slope_c = jnp.where(prow // SUBLANES == h, _alibi_slope(h), slope_c)
    slope_n = jnp.where(nrow // SUBLANES == h, _alibi_slope(h), slope_n)
  bias_c = slope_c * dist_c.astype(F32)
  bias_n = slope_n * dist_n.astype(F32)
  sink = sink_ref[...]
  pad = jnp.zeros((LANES - SUBLANES, KV_W), F32)

  def pair(c, carry):
    r0 = pl.multiple_of(c * SUBLANES, SUBLANES)
    lhs = jnp.concatenate(
        [zq_s[pl.ds(r0, SUBLANES), h * KV_W:(h + 1) * KV_W] for h in range(N_HEADS)],
        axis=0).astype(BF16)
    kc = ck_ref[pl.ds(2 * c, 2)].reshape(2 * WINDOW, KV_W).astype(BF16)
    vc = cv_ref[pl.ds(2 * c, 2)].reshape(2 * WINDOW, KV_W).astype(BF16)
    kn = jnp.concatenate([knew_ref[pl.ds(r0, SUBLANES), :], pad], axis=0).astype(BF16)
    vn = jnp.concatenate([vnew_ref[pl.ds(r0, SUBLANES), :], pad], axis=0).astype(BF16)
    s_c = jnp.where(valid_c, _dot_t(lhs, kc) - bias_c, NEG)
    s_n = jnp.where(valid_n, _dot_t(lhs, kn) - bias_n, NEG)
    m = jnp.maximum(
        jnp.maximum(jnp.max(s_c, axis=-1, keepdims=True), jnp.max(s_n, axis=-1, keepdims=True)),
        sink)
    e_c = jnp.exp(s_c - m)
    e_n = jnp.exp(s_n - m)
    denom = (jnp.sum(e_c, axis=-1, keepdims=True) + jnp.sum(e_n, axis=-1, keepdims=True)
             + jnp.exp(sink - m))
    o = (_dot(e_c.astype(BF16), vc) + _dot(e_n.astype(BF16), vn)) * (1.0 / denom)
    for h in range(N_HEADS):
      attn_s[pl.ds(r0, SUBLANES), h * KV_W:(h + 1) * KV_W] = o[h * SUBLANES:(h + 1) * SUBLANES, :]
    return carry

  lax.fori_loop(0, SAMPLE_SEQS // 2, pair, 0)

  tok = lax.broadcasted_iota(jnp.int32, (rows, LRU_W), 0) % n_tok
  cpad = cpad_ref[...]
  xc = convb_ref[...] + convw_ref[CONV_W - 1:CONV_W, :] * xb
  for shift in range(1, CONV_W):
    saved = cpad if shift == CONV_W - 1 else pltpu.roll(cpad, rows - (CONV_W - 1 - shift), axis=0)
    prev = jnp.where(tok >= shift, pltpu.roll(xb, shift, axis=0), saved)
    xc = xc + convw_ref[CONV_W - 1 - shift:CONV_W - shift, :] * prev

  a, b = _lru_coeffs(xc, wa_ref, ba_ref, wi_ref, bi_ref, lam_ref)
  a, b = _scan_within_groups(a, b, n_tok)
  hs = a * h0_ref[...] + b
  hs_ref[...] = hs
  lru = hs * jax.nn.gelu(yb)

  hres = (x + _dot(attn_s[...].astype(BF16), wout_ref[0:SQ_W, :])
          + _dot(lru.astype(BF16), wout_ref[SQ_W:SQ_W + LRU_W, :]) + bout_ref[...])
  y_ref[...] = _Ffn(hres, gffn_ref, wfi_ref, wfo_ref, gfin_ref).finish()


def _sample_call(x, ck, cv, cpad, h0rep, sink_rows, weights):
  n_rows = x.shape[0]
  rows = SAMPLE_SEQS * CONV_W
  n_steps = n_rows // rows
  vm = _whole(pltpu.VMEM)
  row_spec = lambda w: pl.BlockSpec((rows, w), lambda i: (i, 0))
  cache_spec = pl.BlockSpec((SAMPLE_SEQS, WINDOW, KV_W), lambda i: (i, 0, 0))
  in_specs = [vm, row_spec(D_MODEL), cache_spec, cache_spec, row_spec(LRU_W), row_spec(LRU_W)]
  in_specs += [vm] * len(weights)
  out_shape = (
      jax.ShapeDtypeStruct((n_rows, D_MODEL), F32),
      jax.ShapeDtypeStruct((n_rows, KV_W), F32),
      jax.ShapeDtypeStruct((n_rows, KV_W), F32),
      jax.ShapeDtypeStruct((n_rows, LRU_W), F32),
      jax.ShapeDtypeStruct((n_rows, LRU_W), F32),
  )
  out_specs = (row_spec(D_MODEL), row_spec(KV_W), row_spec(KV_W), row_spec(LRU_W), row_spec(LRU_W))
  scratch = [
      pltpu.VMEM((rows, SQ_W), F32),
      pltpu.VMEM((rows, SQ_W), F32),
  ]
  return pl.pallas_call(
      _sample_kernel,
      grid=(n_steps,),
      in_specs=in_specs,
      out_specs=out_specs,
      out_shape=out_shape,
      scratch_shapes=scratch,
      compiler_params=pltpu.CompilerParams(
          dimension_semantics=("arbitrary",),
          vmem_limit_bytes=VMEM_LIMIT_BYTES),
      name="sample_layer",
  )(sink_rows, x, ck, cv, cpad, h0rep, *weights)


def _block_diag_halves(w):
  per = LRU_BLOCKS // 2
  out = jnp.zeros((2, per * LRU_BW, per * LRU_BW), w.dtype)
  for i in range(LRU_BLOCKS):
    j = i % per
    out = out.at[i // per, j * LRU_BW:(j + 1) * LRU_BW, j * LRU_BW:(j + 1) * LRU_BW].set(w[i])
  return out.astype(BF16)


def _head_to_kv_half(cols):
  parts = []
  zero = jnp.zeros(cols.shape[:-1] + (HEAD_DIM,), cols.dtype)
  for h in range(N_HEADS):
    blk = cols[..., h * HEAD_DIM:(h + 1) * HEAD_DIM]
    parts += [blk, zero] if h // GROUP == 0 else [zero, blk]
  return jnp.concatenate(parts, axis=-1)


def kernel(x_prompt, x_sample, cache_k, cache_v, state_conv, state_h, g_mix, w_in, b_in,
           attn_sinks, conv_w, conv_b, w_a, b_a, w_i, b_i, lam, w_out, b_out, g_ffn,
           w_ffn_in, w_ffn_out, g_final):
  depth = g_mix.shape[0]
  assert depth == 1
  batch, seq, _ = x_prompt.shape
  dec_batch, dec_seq, _ = x_sample.shape
  assert seq % PROMPT_TILE == 0 and dec_seq == CONV_W and dec_batch % SAMPLE_SEQS == 0

  row = lambda p: p.reshape(1, -1)
  common = dict(
      convw=conv_w[0], convb=row(conv_b[0]), wa=_block_diag_halves(w_a[0]), ba=row(b_a[0]),
      wi=_block_diag_halves(w_i[0]), bi=row(b_i[0]), lam=row(lam[0]), bout=row(b_out[0]),
      gffn=row(g_ffn[0]), wfi=w_ffn_in[0].astype(BF16), wfo=w_ffn_out[0].astype(BF16),
      gfin=row(g_final))

  def weight_list(win, bin_, wout):
    c = common
    return [row(g_mix[0]), win, bin_, c["convw"], c["convb"], c["wa"], c["ba"], c["wi"], c["bi"],
            c["lam"], wout, c["bout"], c["gffn"], c["wfi"], c["wfo"], c["gfin"]]

  p_weights = weight_list(w_in[0].astype(BF16), row(b_in[0]), w_out[0].astype(BF16))
  y_p, nk_p, nv_p, nconv_p, nh_p = _prompt_call(x_prompt, attn_sinks[0], p_weights)

  win_s = jnp.concatenate([_head_to_kv_half(w_in[0][:, :Q_END]), w_in[0][:, Q_END:]], axis=-1)
  bin_s = jnp.concatenate([_head_to_kv_half(b_in[0][:Q_END]), b_in[0][Q_END:]], axis=-1)
  wout_s = jnp.concatenate([_head_to_kv_half(w_out[0][:ATTN_W].T).T, w_out[0][ATTN_W:]], axis=0)
  s_weights = weight_list(win_s.astype(BF16), row(bin_s), wout_s.astype(BF16))
  n_rows = dec_batch * dec_seq
  cpad = jnp.pad(state_conv[0], ((0, 0), (0, 1), (0, 0))).reshape(n_rows, LRU_W)
  h0rep = jnp.repeat(state_h[0], dec_seq, axis=0)
  sink_rows = jnp.repeat(attn_sinks[0], SUBLANES).reshape(PAIR_ROWS, 1)
  y_s, knew, vnew, xb_s, hs_s = _sample_call(
      x_sample.reshape(n_rows, D_MODEL), cache_k[0].reshape(dec_batch, WINDOW, KV_W),
      cache_v[0].reshape(dec_batch, WINDOW, KV_W), cpad, h0rep, sink_rows, s_weights)

  kv_shape = (dec_batch, dec_seq, N_KV_HEADS, HEAD_DIM)
  nk_s = jnp.concatenate([cache_k[0][:, dec_seq:], knew.reshape(kv_shape)], axis=1)
  nv_s = jnp.concatenate([cache_v[0][:, dec_seq:], vnew.reshape(kv_shape)], axis=1)
  nconv_s = xb_s.reshape(dec_batch, dec_seq, LRU_W)[:, dec_seq - (CONV_W - 1):]
  nh_s = hs_s.reshape(dec_batch, dec_seq, LRU_W)[:, dec_seq - 1]

  pkv = (1, batch, WINDOW, N_KV_HEADS, HEAD_DIM)
  return (y_p, y_s.reshape(dec_batch, dec_seq, D_MODEL),
          nk_p.reshape(pkv), nv_p.reshape(pkv), nconv_p[None], nh_p.reshape(1, batch, LRU_W),
          nk_s[None], nv_s[None], nconv_s[None], nh_s[None])
```

```python
import functools

import jax
import jax.numpy as jnp
from jax import lax
from jax.experimental import pallas as pl
from jax.experimental.pallas import tpu as pltpu

D_MODEL = 1024
N_HEADS = 8
N_KV_HEADS = 2
HEAD_DIM = 64
GROUP = N_HEADS // N_KV_HEADS
ATTN_W = N_HEADS * HEAD_DIM
KV_W = N_KV_HEADS * HEAD_DIM
WINDOW = 128
LRU_W = D_MODEL // 2
LRU_BLOCKS = 8
LRU_BW = LRU_W // LRU_BLOCKS
CONV_W = 4
LRU_C = 8.0
IN_W = ATTN_W + 2 * KV_W + 2 * LRU_W
Q_END = ATTN_W
K_END = Q_END + KV_W
V_END = K_END + KV_W
X_END = V_END + LRU_W
D_FF = 2816
EPS = 1e-6
NEG = -1e30

SUBLANES = 8
LANES = 128
VMEM_LIMIT_BYTES = 56 * 1024 * 1024

PROMPT_TILE = 512
FFN_CHUNK = 256
LRU_ROWS = 128
N_PAIRS = N_HEADS // 2
SAMPLE_SEQS = 32
PAIR_ROWS = N_HEADS * SUBLANES
SQ_W = N_HEADS * KV_W

F32 = jnp.float32
BF16 = jnp.bfloat16


def _dot(a, b):
  return jnp.dot(a, b, preferred_element_type=F32)


def _dot_t(a, b):
  return lax.dot_general(a, b, (((1,), (1,)), ((), ())), preferred_element_type=F32)


def _rmsnorm(x, g):
  return x * lax.rsqrt(jnp.mean(x * x, axis=-1, keepdims=True) + EPS) * g


def _alibi_slope(h):
  return 2.0 ** (-8.0 * (h + 1) / N_HEADS)


def _lru_coeffs(xc, wa_ref, ba_ref, wi_ref, bi_ref, lam_ref):
  xcb = xc.astype(BF16)
  half = LRU_W // 2
  lo, hi = xcb[:, :half], xcb[:, half:]
  r = jnp.concatenate([_dot(lo, wa_ref[0]), _dot(hi, wa_ref[1])], axis=-1) + ba_ref[...]
  ig = jnp.concatenate([_dot(lo, wi_ref[0]), _dot(hi, wi_ref[1])], axis=-1) + bi_ref[...]
  r = jax.nn.sigmoid(r)
  ig = jax.nn.sigmoid(ig)
  nlam = -lam_ref[...]
  softplus = jnp.maximum(nlam, 0.0) + jnp.log1p(jnp.exp(-jnp.abs(nlam)))
  log_a = -LRU_C * r * softplus
  a = jnp.exp(log_a)
  gain_sq = -jnp.tanh(log_a) * (a * a + 1.0)
  gain = jnp.where(gain_sq > 0.0, gain_sq * lax.rsqrt(gain_sq), 0.0)
  return a, gain * (ig * xc)


def _scan_within_groups(a, b, period):
  t, w = a.shape
  a3 = a.reshape(t // SUBLANES, SUBLANES, w)
  b3 = b.reshape(t // SUBLANES, SUBLANES, w)
  pos = lax.broadcasted_iota(jnp.int32, (1, SUBLANES, w), 1) % period
  d = 1
  while d < period:
    keep = pos >= d
    a_prev = jnp.where(keep, pltpu.roll(a3, d, axis=1), 1.0)
    b_prev = jnp.where(keep, pltpu.roll(b3, d, axis=1), 0.0)
    b3 = a3 * b_prev + b3
    a3 = a3 * a_prev
    d *= 2
  return a3.reshape(t, w), b3.reshape(t, w)


class _Ffn:
  def __init__(self, hres, hn, wfi_ref, wfo_ref, gfin_ref, act_ref):
    self.hres, self.hn = hres, hn
    self.wfi_ref, self.wfo_ref, self.gfin_ref, self.act_ref = wfi_ref, wfo_ref, gfin_ref, act_ref
    self.n_up = 2 * (D_FF // FFN_CHUNK)
    self.n_units = self.n_up + D_MODEL // FFN_CHUNK
    self.done = 0
    self.gate = None
    self.pieces = []

  def run(self, n_units=1):
    for _ in range(min(n_units, self.n_units - self.done)):
      if self.done < self.n_up:
        lo = (self.done // 2) * FFN_CHUNK
        if self.done % 2 == 0:
          self.gate = _dot(self.hn, self.wfi_ref[:, lo:lo + FFN_CHUNK])
        else:
          up = _dot(self.hn, self.wfi_ref[:, D_FF + lo:D_FF + lo + FFN_CHUNK])
          self.act_ref[:, lo:lo + FFN_CHUNK] = (jax.nn.silu(self.gate) * up).astype(BF16)
      else:
        lo = (self.done - self.n_up) * FFN_CHUNK
        self.pieces.append(self.hres[:, lo:lo + FFN_CHUNK]
                           + _dot(self.act_ref[...], self.wfo_ref[:, lo:lo + FFN_CHUNK]))
      self.done += 1

  def finish(self):
    self.run(self.n_units)
    return _rmsnorm(jnp.concatenate(self.pieces, axis=-1), self.gfin_ref[...])


def _prompt_kernel(
    sinks_ref, x_ref, gmix_ref, win_ref, bin_ref, convw_ref, convb_ref, wa_ref, ba_ref,
    wi_ref, bi_ref, lam_ref, wout_ref, bout_ref, gffn_ref, wfi_ref, wfo_ref, gfin_ref,
    y_ref, nk_ref, nv_ref, nconv_ref, nh_ref,
    q_s, kz_s, vz_s, xpad_s, hs_s, hcar_s, attn_s, lru_s, hres_s, hn_s, act_s, bias_s, *, tiles_per_seq):
  tq = PROMPT_TILE
  step = pl.program_id(0)
  slot = step % 2
  t_idx = step % tiles_per_seq
  first_tile = t_idx == 0
  last_tile = t_idx == tiles_per_seq - 1

  row = lax.broadcasted_iota(jnp.int32, (WINDOW, 2 * WINDOW), 0)
  col = lax.broadcasted_iota(jnp.int32, (WINDOW, 2 * WINDOW), 1)
  dist = WINDOW + row - col

  @pl.when(step == 0)
  def _():
    hres_s[...] = jnp.zeros((2, tq, D_MODEL), F32)
    hn_s[...] = jnp.zeros((tq, D_MODEL), BF16)
    for h in range(N_HEADS):
      bias_s[h] = _alibi_slope(h) * dist.astype(F32)

  @pl.when(first_tile)
  def _():
    kz_s[:, 0:WINDOW, :] = jnp.zeros((4, WINDOW, KV_W), BF16)
    vz_s[:, 0:WINDOW, :] = jnp.zeros((4, WINDOW, KV_W), BF16)
    xpad_s[0:SUBLANES, :] = jnp.zeros((SUBLANES, LRU_W), F32)
    hcar_s[...] = jnp.zeros((SUBLANES, LRU_W), F32)

  ffn = _Ffn(hres_s[1 - slot], hn_s[...], wfi_ref, wfo_ref, gfin_ref, act_s)
  ffn.run(2)

  x = x_ref[0]
  u = _rmsnorm(x, gmix_ref[...]).astype(BF16)

  q = _dot(u, win_ref[:, 0:Q_END]) + bin_ref[:, 0:Q_END]
  q_s[...] = (q * (HEAD_DIM ** -0.5)).astype(BF16)
  kv = _dot(u, win_ref[:, Q_END:V_END]) + bin_ref[:, Q_END:V_END]
  k = kv[:, 0:KV_W]
  v = kv[:, KV_W:2 * KV_W]
  lru_in = {}

  def project_lru_section(name, lo):
    lru_in[name] = _dot(u, win_ref[:, lo:lo + LRU_W]) + bin_ref[:, lo:lo + LRU_W]

  fillers = [functools.partial(project_lru_section, "xb", V_END),
             functools.partial(project_lru_section, "yb", X_END)]

  def filler():
    if fillers:
      fillers.pop(0)()
    else:
      ffn.run(1)

  lane = lax.broadcasted_iota(jnp.int32, (tq, KV_W), 1)
  is_lo = lane < HEAD_DIM
  for src, dst in ((k, kz_s), (v, vz_s)):
    swapped = pltpu.roll(src, HEAD_DIM, axis=1)
    dst[0, WINDOW:WINDOW + tq, :] = jnp.where(is_lo, src, 0.0).astype(BF16)
    dst[1, WINDOW:WINDOW + tq, :] = jnp.where(is_lo, 0.0, swapped).astype(BF16)
    dst[2, WINDOW:WINDOW + tq, :] = jnp.where(is_lo, swapped, 0.0).astype(BF16)
    dst[3, WINDOW:WINDOW + tq, :] = jnp.where(is_lo, 0.0, src).astype(BF16)

  own_ok = (col >= WINDOW) & (dist >= 0)
  prev_dist = jnp.where(col < WINDOW, dist, 2 * WINDOW)
  lane_q = lax.broadcasted_iota(jnp.int32, (WINDOW, KV_W), 1)

  def operand(ref, j, p):
    g = p // (N_PAIRS // N_KV_HEADS)
    r0 = j * WINDOW
    return jnp.concatenate(
        [ref[2 * g, r0:r0 + 2 * WINDOW, :], ref[2 * g + 1, r0:r0 + 2 * WINDOW, :]], axis=0)

  def scores(j, p):
    r0 = j * WINDOW
    return _dot_t(q_s[r0:r0 + WINDOW, p * KV_W:(p + 1) * KV_W], operand(kz_s, j, p))

  def softmax(s, j, p):
    prev_limit = jnp.where(first_tile, 0, WINDOW) if j == 0 else WINDOW
    valid = own_ok | (prev_dist < prev_limit)
    probs, inv = [], []
    for hh in range(2):
      h = 2 * p + hh
      sink = sinks_ref[h]
      sh = s[:, hh * 2 * WINDOW:(hh + 1) * 2 * WINDOW]
      sh = jnp.where(valid, sh - bias_s[h], NEG)
      m = jnp.maximum(jnp.max(sh, axis=-1, keepdims=True), sink)
      e = jnp.exp(sh - m)
      denom = jnp.sum(e, axis=-1, keepdims=True) + jnp.exp(sink - m)
      probs.append(e.astype(BF16))
      inv.append(1.0 / denom)
    return jnp.concatenate(probs, axis=-1), jnp.where(lane_q < HEAD_DIM, inv[0], inv[1])

  def weighted_values(j, p, probs, inv):
    r0 = j * WINDOW
    o = _dot(probs, operand(vz_s, j, p)) * inv
    attn_s[r0:r0 + WINDOW, p * KV_W:(p + 1) * KV_W] = o.astype(BF16)

  problems = [(j, p) for j in range(tq // WINDOW) for p in range(N_PAIRS)]
  s_next = scores(*problems[0])
  for i, (j, p) in enumerate(problems):
    s_cur = s_next
    if i + 1 < len(problems):
      s_next = scores(*problems[i + 1])
    probs, inv = softmax(s_cur, j, p)
    filler()
    weighted_values(j, p, probs, inv)

  kz_s[:, 0:WINDOW, :] = kz_s[:, tq:tq + WINDOW, :]
  vz_s[:, 0:WINDOW, :] = vz_s[:, tq:tq + WINDOW, :]

  xb, yb = lru_in["xb"], lru_in["yb"]
  xpad_s[SUBLANES:SUBLANES + tq, :] = xb
  h = hcar_s[...]
  for rc in range(tq // LRU_ROWS):
    c0 = rc * LRU_ROWS
    xc = convb_ref[...] + convw_ref[CONV_W - 1:CONV_W, :] * xb[c0:c0 + LRU_ROWS, :]
    for kk in range(CONV_W - 1):
      off = c0 + SUBLANES - (CONV_W - 1) + kk
      xc = xc + convw_ref[kk:kk + 1, :] * xpad_s[off:off + LRU_ROWS, :]
    a, b = _lru_coeffs(xc, wa_ref, ba_ref, wi_ref, bi_ref, lam_ref)
    filler()
    a, b = _scan_within_groups(a, b, SUBLANES)
    for gi in range(LRU_ROWS // SUBLANES):
      r0 = gi * SUBLANES
      hg = a[r0:r0 + SUBLANES, :] * h + b[r0:r0 + SUBLANES, :]
      hs_s[c0 + r0:c0 + r0 + SUBLANES, :] = hg
      h = jnp.broadcast_to(hg[SUBLANES - 1:SUBLANES, :], (SUBLANES, LRU_W))
    lru_s[c0:c0 + LRU_ROWS, :] = (
        hs_s[c0:c0 + LRU_ROWS, :] * jax.nn.gelu(yb[c0:c0 + LRU_ROWS, :])).astype(BF16)
    filler()
  xpad_s[0:SUBLANES, :] = xpad_s[tq:tq + SUBLANES, :]
  hcar_s[...] = h

  @pl.when(last_tile)
  def _():
    nk_ref[0] = k[tq - WINDOW:tq, :]
    nv_ref[0] = v[tq - WINDOW:tq, :]
    nconv_ref[0] = xb[tq - (CONV_W - 1):tq, :]
    nh_ref[0] = h[0:1, :]

  hres = (x + _dot(attn_s[...], wout_ref[0:ATTN_W, :])
          + _dot(lru_s[...], wout_ref[ATTN_W:ATTN_W + LRU_W, :]) + bout_ref[...])
  y_ref[0] = ffn.finish()
  hres_s[slot] = hres
  hn_s[...] = _rmsnorm(hres, gffn_ref[...]).astype(BF16)


def _whole(space):
  return pl.BlockSpec(memory_space=space)


def _prompt_call(x, sinks, weights):
  batch, seq, _ = x.shape
  tq = PROMPT_TILE
  n_t = seq // tq
  n_tiles = batch * n_t
  vm = _whole(pltpu.VMEM)

  def mixer_tile(s):
    return jnp.minimum(s, n_tiles - 1)

  def ffn_tile(s):
    return jnp.maximum(s - 1, 0)

  in_specs = [_whole(pltpu.SMEM),
              pl.BlockSpec((1, tq, D_MODEL), lambda s: (mixer_tile(s) // n_t, mixer_tile(s) % n_t, 0))]
  in_specs += [vm] * len(weights)
  out_shape = (
      jax.ShapeDtypeStruct((batch, seq, D_MODEL), F32),
      jax.ShapeDtypeStruct((batch, WINDOW, KV_W), F32),
      jax.ShapeDtypeStruct((batch, WINDOW, KV_W), F32),
      jax.ShapeDtypeStruct((batch, CONV_W - 1, LRU_W), F32),
      jax.ShapeDtypeStruct((batch, 1, LRU_W), F32),
  )
  state_map = lambda s: (mixer_tile(s) // n_t, 0, 0)
  out_specs = (
      pl.BlockSpec((1, tq, D_MODEL), lambda s: (ffn_tile(s) // n_t, ffn_tile(s) % n_t, 0)),
      pl.BlockSpec((1, WINDOW, KV_W), state_map),
      pl.BlockSpec((1, WINDOW, KV_W), state_map),
      pl.BlockSpec((1, CONV_W - 1, LRU_W), state_map),
      pl.BlockSpec((1, 1, LRU_W), state_map),
  )
  scratch = [
      pltpu.VMEM((tq, ATTN_W), BF16),
      pltpu.VMEM((4, WINDOW + tq, KV_W), BF16),
      pltpu.VMEM((4, WINDOW + tq, KV_W), BF16),
      pltpu.VMEM((SUBLANES + tq, LRU_W), F32),
      pltpu.VMEM((tq, LRU_W), F32),
      pltpu.VMEM((SUBLANES, LRU_W), F32),
      pltpu.VMEM((tq, ATTN_W), BF16),
      pltpu.VMEM((tq, LRU_W), BF16),
      pltpu.VMEM((2, tq, D_MODEL), F32),
      pltpu.VMEM((tq, D_MODEL), BF16),
      pltpu.VMEM((tq, D_FF), BF16),
      pltpu.VMEM((N_HEADS, WINDOW, 2 * WINDOW), F32),
  ]
  return pl.pallas_call(
      functools.partial(_prompt_kernel, tiles_per_seq=n_t),
      grid=(n_tiles + 1,),
      in_specs=in_specs,
      out_specs=out_specs,
      out_shape=out_shape,
      scratch_shapes=scratch,
      compiler_params=pltpu.CompilerParams(
          dimension_semantics=("arbitrary",),
          vmem_limit_bytes=VMEM_LIMIT_BYTES),
      name="prompt_layer",
  )(sinks, x, *weights)


def _sample_kernel(
    sink_ref, x_ref, ck_ref, cv_ref, cpad_ref, h0_ref, gmix_ref, win_ref, bin_ref, convw_ref,
    convb_ref, wa_ref, ba_ref, wi_ref, bi_ref, lam_ref, wout_ref, bout_ref, gffn_ref, wfi_ref,
    wfo_ref, gfin_ref,
    y_ref, knew_ref, vnew_ref, xb_ref, hs_ref,
    zq_s, attn_s, act_s):
  rows = SAMPLE_SEQS * CONV_W
  n_tok = CONV_W
  x = x_ref[...]
  u = _rmsnorm(x, gmix_ref[...]).astype(BF16)

  zq_s[...] = (_dot(u, win_ref[:, 0:SQ_W]) + bin_ref[:, 0:SQ_W]) * (HEAD_DIM ** -0.5)
  kv = _dot(u, win_ref[:, SQ_W:SQ_W + 2 * KV_W]) + bin_ref[:, SQ_W:SQ_W + 2 * KV_W]
  knew_ref[...] = kv[:, 0:KV_W]
  vnew_ref[...] = kv[:, KV_W:2 * KV_W]
  xs = SQ_W + 2 * KV_W
  xb = _dot(u, win_ref[:, xs:xs + LRU_W]) + bin_ref[:, xs:xs + LRU_W]
  yb = _dot(u, win_ref[:, xs + LRU_W:xs + 2 * LRU_W]) + bin_ref[:, xs + LRU_W:xs + 2 * LRU_W]
  xb_ref[...] = xb

  prow = lax.broadcasted_iota(jnp.int32, (PAIR_ROWS, 2 * WINDOW), 0)
  pcol = lax.broadcasted_iota(jnp.int32, (PAIR_ROWS, 2 * WINDOW), 1)
  seq_r = (prow % SUBLANES) // n_tok
  tok_r = prow % n_tok
  dist_c = WINDOW + tok_r - (pcol % WINDOW)
  valid_c = (seq_r == pcol // WINDOW) & (dist_c >= 0) & (dist_c < WINDOW)
  nrow = lax.broadcasted_iota(jnp.int32, (PAIR_ROWS, LANES), 0)
  ncol = lax.broadcasted_iota(jnp.int32, (PAIR_ROWS, LANES), 1)
  dist_n = (nrow % n_tok) - (ncol % n_tok)
  valid_n = (ncol < SUBLANES) & ((nrow % SUBLANES) // n_tok == ncol // n_tok) & (dist_n >= 0)
  slope_c = jnp.zeros((PAIR_ROWS, 2 * WINDOW), F32)
  slope_n = jnp.zeros((PAIR_ROWS, LANES), F32)
  for h in range(N_HEADS):
    slope_c = jnp.where(prow // SUBLANES == h, _alibi_slope(h), slope_c)
    slope_n = jnp.where(nrow // SUBLANES == h, _alibi_slope(h), slope_n)
  bias_c = slope_c * dist_c.astype(F32)
  bias_n = slope_n * dist_n.astype(F32)
  sink = sink_ref[...]
  pad = jnp.zeros((LANES - SUBLANES, KV_W), F32)

  def pair(c, carry):
    r0 = pl.multiple_of(c * SUBLANES, SUBLANES)
    lhs = jnp.concatenate(
        [zq_s[pl.ds(r0, SUBLANES), h * KV_W:(h + 1) * KV_W] for h in range(N_HEADS)],
        axis=0).astype(BF16)
    kc = ck_ref[pl.ds(2 * c, 2)].reshape(2 * WINDOW, KV_W).astype(BF16)
    vc = cv_ref[pl.ds(2 * c, 2)].reshape(2 * WINDOW, KV_W).astype(BF16)
    kn = jnp.concatenate([knew_ref[pl.ds(r0, SUBLANES), :], pad], axis=0).astype(BF16)
    vn = jnp.concatenate([vnew_ref[pl.ds(r0, SUBLANES), :], pad], axis=0).astype(BF16)
    s_c = jnp.where(valid_c, _dot_t(lhs, kc) - bias_c, NEG)
    s_n = jnp.where(valid_n, _dot_t(lhs, kn) - bias_n, NEG)
    m = jnp.maximum(
        jnp.maximum(jnp.max(s_c, axis=-1, keepdims=True), jnp.max(s_n, axis=-1, keepdims=True)),
        sink)
    e_c = jnp.exp(s_c - m)
    e_n = jnp.exp(s_n - m)
    denom = (jnp.sum(e_c, axis=-1, keepdims=True) + jnp.sum(e_n, axis=-1, keepdims=True)
             + jnp.exp(sink - m))
    o = (_dot(e_c.astype(BF16), vc) + _dot(e_n.astype(BF16), vn)) * (1.0 / denom)
    for h in range(N_HEADS):
      attn_s[pl.ds(r0, SUBLANES), h * KV_W:(h + 1) * KV_W] = o[h * SUBLANES:(h + 1) * SUBLANES, :]
    return carry

  lax.fori_loop(0, SAMPLE_SEQS // 2, pair, 0)

  tok = lax.broadcasted_iota(jnp.int32, (rows, LRU_W), 0) % n_tok
  cpad = cpad_ref[...]
  xc = convb_ref[...] + convw_ref[CONV_W - 1:CONV_W, :] * xb
  for shift in range(1, CONV_W):
    saved = cpad if shift == CONV_W - 1 else pltpu.roll(cpad, rows - (CONV_W - 1 - shift), axis=0)
    prev = jnp.where(tok >= shift, pltpu.roll(xb, shift, axis=0), saved)
    xc = xc + convw_ref[CONV_W - 1 - shift:CONV_W - shift, :] * prev

  a, b = _lru_coeffs(xc, wa_ref, ba_ref, wi_ref, bi_ref, lam_ref)
  a, b = _scan_within_groups(a, b, n_tok)
  hs = a * h0_ref[...] + b
  hs_ref[...] = hs
  lru = hs * jax.nn.gelu(yb)

  hres = (x + _dot(attn_s[...].astype(BF16), wout_ref[0:SQ_W, :])
          + _dot(lru.astype(BF16), wout_ref[SQ_W:SQ_W + LRU_W, :]) + bout_ref[...])
  hn = _rmsnorm(hres, gffn_ref[...]).astype(BF16)
  y_ref[...] = _Ffn(hres, hn, wfi_ref, wfo_ref, gfin_ref, act_s).finish()


def _sample_call(x, ck, cv, cpad, h0rep, sink_rows, weights):
  n_rows = x.shape[0]
  rows = SAMPLE_SEQS * CONV_W
  n_steps = n_rows // rows
  vm = _whole(pltpu.VMEM)
  row_spec = lambda w: pl.BlockSpec((rows, w), lambda i: (i, 0))
  cache_spec = pl.BlockSpec((SAMPLE_SEQS, WINDOW, KV_W), lambda i: (i, 0, 0))
  in_specs = [vm, row_spec(D_MODEL), cache_spec, cache_spec, row_spec(LRU_W), row_spec(LRU_W)]
  in_specs += [vm] * len(weights)
  out_shape = (
      jax.ShapeDtypeStruct((n_rows, D_MODEL), F32),
      jax.ShapeDtypeStruct((n_rows, KV_W), F32),
      jax.ShapeDtypeStruct((n_rows, KV_W), F32),
      jax.ShapeDtypeStruct((n_rows, LRU_W), F32),
      jax.ShapeDtypeStruct((n_rows, LRU_W), F32),
  )
  out_specs = (row_spec(D_MODEL), row_spec(KV_W), row_spec(KV_W), row_spec(LRU_W), row_spec(LRU_W))
  scratch = [
      pltpu.VMEM((rows, SQ_W), F32),
      pltpu.VMEM((rows, SQ_W), F32),
      pltpu.VMEM((rows, D_FF), BF16),
  ]
  return pl.pallas_call(
      _sample_kernel,
      grid=(n_steps,),
      in_specs=in_specs,
      out_specs=out_specs,
      out_shape=out_shape,
      scratch_shapes=scratch,
      compiler_params=pltpu.CompilerParams(
          dimension_semantics=("arbitrary",),
          vmem_limit_bytes=VMEM_LIMIT_BYTES),
      name="sample_layer",
  )(sink_rows, x, ck, cv, cpad, h0rep, *weights)


def _block_diag_halves(w):
  per = LRU_BLOCKS // 2
  out = jnp.zeros((2, per * LRU_BW, per * LRU_BW), w.dtype)
  for i in range(LRU_BLOCKS):
    j = i % per
    out = out.at[i // per, j * LRU_BW:(j + 1) * LRU_BW, j * LRU_BW:(j + 1) * LRU_BW].set(w[i])
  return out.astype(BF16)


def _head_to_kv_half(cols):
  parts = []
  zero = jnp.zeros(cols.shape[:-1] + (HEAD_DIM,), cols.dtype)
  for h in range(N_HEADS):
    blk = cols[..., h * HEAD_DIM:(h + 1) * HEAD_DIM]
    parts += [blk, zero] if h // GROUP == 0 else [zero, blk]
  return jnp.concatenate(parts, axis=-1)


def kernel(x_prompt, x_sample, cache_k, cache_v, state_conv, state_h, g_mix, w_in, b_in,
           attn_sinks, conv_w, conv_b, w_a, b_a, w_i, b_i, lam, w_out, b_out, g_ffn,
           w_ffn_in, w_ffn_out, g_final):
  depth = g_mix.shape[0]
  assert depth == 1
  batch, seq, _ = x_prompt.shape
  dec_batch, dec_seq, _ = x_sample.shape
  assert seq % PROMPT_TILE == 0 and dec_seq == CONV_W and dec_batch % SAMPLE_SEQS == 0

  row = lambda p: p.reshape(1, -1)
  common = dict(
      convw=conv_w[0], convb=row(conv_b[0]), wa=_block_diag_halves(w_a[0]), ba=row(b_a[0]),
      wi=_block_diag_halves(w_i[0]), bi=row(b_i[0]), lam=row(lam[0]), bout=row(b_out[0]),
      gffn=row(g_ffn[0]), wfi=w_ffn_in[0].astype(BF16), wfo=w_ffn_out[0].astype(BF16),
      gfin=row(g_final))

  def weight_list(win, bin_, wout):
    c = common
    return [row(g_mix[0]), win, bin_, c["convw"], c["convb"], c["wa"], c["ba"], c["wi"], c["bi"],
            c["lam"], wout, c["bout"], c["gffn"], c["wfi"], c["wfo"], c["gfin"]]

  p_weights = weight_list(w_in[0].astype(BF16), row(b_in[0]), w_out[0].astype(BF16))
  y_p, nk_p, nv_p, nconv_p, nh_p = _prompt_call(x_prompt, attn_sinks[0], p_weights)

  win_s = jnp.concatenate([_head_to_kv_half(w_in[0][:, :Q_END]), w_in[0][:, Q_END:]], axis=-1)
  bin_s = jnp.concatenate([_head_to_kv_half(b_in[0][:Q_END]), b_in[0][Q_END:]], axis=-1)
  wout_s = jnp.concatenate([_head_to_kv_half(w_out[0][:ATTN_W].T).T, w_out[0][ATTN_W:]], axis=0)
  s_weights = weight_list(win_s.astype(BF16), row(bin_s), wout_s.astype(BF16))
  n_rows = dec_batch * dec_seq
  cpad = jnp.pad(state_conv[0], ((0, 0), (0, 1), (0, 0))).reshape(n_rows, LRU_W)
  h0rep = jnp.repeat(state_h[0], dec_seq, axis=0)
  sink_rows = jnp.repeat(attn_sinks[0], SUBLANES).reshape(PAIR_ROWS, 1)
  y_s, knew, vnew, xb_s, hs_s = _sample_call(
      x_sample.reshape(n_rows, D_MODEL), cache_k[0].reshape(dec_batch, WINDOW, KV_W),
      cache_v[0].reshape(dec_batch, WINDOW, KV_W), cpad, h0rep, sink_rows, s_weights)

  kv_shape = (dec_batch, dec_seq, N_KV_HEADS, HEAD_DIM)
  nk_s = jnp.concatenate([cache_k[0][:, dec_seq:], knew.reshape(kv_shape)], axis=1)
  nv_s = jnp.concatenate([cache_v[0][:, dec_seq:], vnew.reshape(kv_shape)], axis=1)
  nconv_s = xb_s.reshape(dec_batch, dec_seq, LRU_W)[:, dec_seq - (CONV_W - 1):]
  nh_s = hs_s.reshape(dec_batch, dec_seq, LRU_W)[:, dec_seq - 1]

  pkv = (1, batch, WINDOW, N_KV_HEADS, HEAD_DIM)
  return (y_p, y_s.reshape(dec_batch, dec_seq, D_MODEL),
          nk_p.reshape(pkv), nv_p.reshape(pkv), nconv_p[None], nh_p.reshape(1, batch, LRU_W),
          nk_s[None], nv_s[None], nconv_s[None], nh_s[None])
```

```python
import functools

import jax
import jax.numpy as jnp
from jax import lax
from jax.experimental import pallas as pl
from jax.experimental.pallas import tpu as pltpu

D_MODEL = 1024
N_HEADS = 8
N_KV_HEADS = 2
HEAD_DIM = 64
GROUP = N_HEADS // N_KV_HEADS
ATTN_W = N_HEADS * HEAD_DIM
KV_W = N_KV_HEADS * HEAD_DIM
WINDOW = 128
LRU_W = D_MODEL // 2
LRU_BLOCKS = 8
LRU_BW = LRU_W // LRU_BLOCKS
CONV_W = 4
LRU_C = 8.0
IN_W = ATTN_W + 2 * KV_W + 2 * LRU_W
Q_END = ATTN_W
K_END = Q_END + KV_W
V_END = K_END + KV_W
X_END = V_END + LRU_W
D_FF = 2816
EPS = 1e-6
NEG = -1e30

SUBLANES = 8
LANES = 128
VMEM_LIMIT_BYTES = 56 * 1024 * 1024

PROMPT_TILE = 512
FFN_CHUNK = 256
LRU_ROWS = WINDOW
N_PAIRS = N_HEADS // 2
SAMPLE_SEQS = 16
PAIR_ROWS = N_HEADS * SUBLANES
SAMPLE_PAIRS_PER_ITER = 4

F32 = jnp.float32
BF16 = jnp.bfloat16


def _dot(a, b):
  return jnp.dot(a, b, preferred_element_type=F32)


def _dot_t(a, b):
  return lax.dot_general(a, b, (((1,), (1,)), ((), ())), preferred_element_type=F32)


def _rmsnorm(x, g):
  return x * lax.rsqrt(jnp.mean(x * x, axis=-1, keepdims=True) + EPS) * g


def _alibi_slope(h):
  return 2.0 ** (-8.0 * (h + 1) / N_HEADS)


def _lru_coeffs(xc, wa_ref, ba_ref, wi_ref, bi_ref, lam_ref):
  xcb = xc.astype(BF16)
  half = LRU_W // 2
  lo, hi = xcb[:, :half], xcb[:, half:]
  r = jnp.concatenate([_dot(lo, wa_ref[0]), _dot(hi, wa_ref[1])], axis=-1) + ba_ref[...]
  ig = jnp.concatenate([_dot(lo, wi_ref[0]), _dot(hi, wi_ref[1])], axis=-1) + bi_ref[...]
  r = jax.nn.sigmoid(r)
  ig = jax.nn.sigmoid(ig)
  nlam = -lam_ref[...]
  softplus = jnp.maximum(nlam, 0.0) + jnp.log1p(jnp.exp(-jnp.abs(nlam)))
  log_a = -LRU_C * r * softplus
  a = jnp.exp(log_a)
  gain_sq = -jnp.tanh(log_a) * (a * a + 1.0)
  gain = jnp.where(gain_sq > 0.0, gain_sq * lax.rsqrt(gain_sq), 0.0)
  return a, gain * (ig * xc)


def _scan_within_groups(a, b, period):
  t, w = a.shape
  a3 = a.reshape(t // SUBLANES, SUBLANES, w)
  b3 = b.reshape(t // SUBLANES, SUBLANES, w)
  pos = lax.broadcasted_iota(jnp.int32, (1, SUBLANES, w), 1) % period
  d = 1
  while d < period:
    keep = pos >= d
    a_prev = jnp.where(keep, pltpu.roll(a3, d, axis=1), 1.0)
    b_prev = jnp.where(keep, pltpu.roll(b3, d, axis=1), 0.0)
    b3 = a3 * b_prev + b3
    a3 = a3 * a_prev
    d *= 2
  return a3.reshape(t, w), b3.reshape(t, w)


class _Ffn:
  def __init__(self, hres, hn, wfi_ref, wfo_ref, gfin_ref, act_ref):
    self.hres, self.hn = hres, hn
    self.wfi_ref, self.wfo_ref, self.gfin_ref, self.act_ref = wfi_ref, wfo_ref, gfin_ref, act_ref
    self.n_up = 2 * (D_FF // FFN_CHUNK)
    self.n_units = self.n_up + D_MODEL // FFN_CHUNK
    self.done = 0
    self.gate = None
    self.pieces = []

  def run(self, n_units=1):
    for _ in range(min(n_units, self.n_units - self.done)):
      if self.done < self.n_up:
        lo = (self.done // 2) * FFN_CHUNK
        if self.done % 2 == 0:
          self.gate = _dot(self.hn, self.wfi_ref[:, lo:lo + FFN_CHUNK])
        else:
          up = _dot(self.hn, self.wfi_ref[:, D_FF + lo:D_FF + lo + FFN_CHUNK])
          self.act_ref[:, lo:lo + FFN_CHUNK] = (jax.nn.silu(self.gate) * up).astype(BF16)
      else:
        lo = (self.done - self.n_up) * FFN_CHUNK
        self.pieces.append(self.hres[:, lo:lo + FFN_CHUNK]
                           + _dot(self.act_ref[...], self.wfo_ref[:, lo:lo + FFN_CHUNK]))
      self.done += 1

  def finish(self):
    self.run(self.n_units)
    return _rmsnorm(jnp.concatenate(self.pieces, axis=-1), self.gfin_ref[...])


def _prompt_kernel(
    sinks_ref, x_ref, gmix_ref, win_ref, bin_ref, convw_ref, convb_ref, wa_ref, ba_ref,
    wi_ref, bi_ref, lam_ref, wout_ref, bout_ref, gffn_ref, wfi_ref, wfo_ref, gfin_ref,
    y_ref, nk_ref, nv_ref, nconv_ref, nh_ref,
    q_s, kz_s, vz_s, xpad_s, hs_s, hcar_s, attn_s, lru_s, hres_s, hn_s, act_s, bias_s, *, tiles_per_seq):
  tq = PROMPT_TILE
  step = pl.program_id(0)
  slot = step % 2
  t_idx = step % tiles_per_seq
  first_tile = t_idx == 0
  last_tile = t_idx == tiles_per_seq - 1

  row = lax.broadcasted_iota(jnp.int32, (WINDOW, 2 * WINDOW), 0)
  col = lax.broadcasted_iota(jnp.int32, (WINDOW, 2 * WINDOW), 1)
  dist = WINDOW + row - col

  @pl.when(step == 0)
  def _():
    hres_s[...] = jnp.zeros((2, tq, D_MODEL), F32)
    hn_s[...] = jnp.zeros((tq, D_MODEL), BF16)
    for h in range(N_HEADS):
      bias_s[h] = _alibi_slope(h) * dist.astype(F32)

  @pl.when(first_tile)
  def _():
    kz_s[:, 0:WINDOW, :] = jnp.zeros((4, WINDOW, KV_W), BF16)
    vz_s[:, 0:WINDOW, :] = jnp.zeros((4, WINDOW, KV_W), BF16)
    xpad_s[0:SUBLANES, :] = jnp.zeros((SUBLANES, LRU_W), F32)
    hcar_s[...] = jnp.zeros((SUBLANES, LRU_W), F32)

  ffn = _Ffn(hres_s[1 - slot], hn_s[...], wfi_ref, wfo_ref, gfin_ref, act_s)
  ffn.run(2)

  x = x_ref[0]
  u = _rmsnorm(x, gmix_ref[...]).astype(BF16)

  q = _dot(u, win_ref[:, 0:Q_END]) + bin_ref[:, 0:Q_END]
  q_s[...] = (q * (HEAD_DIM ** -0.5)).astype(BF16)
  kv = _dot(u, win_ref[:, Q_END:V_END]) + bin_ref[:, Q_END:V_END]
  k = kv[:, 0:KV_W]
  v = kv[:, KV_W:2 * KV_W]
  lru_in = {}

  def project_lru_section(name, lo):
    lru_in[name] = _dot(u, win_ref[:, lo:lo + LRU_W]) + bin_ref[:, lo:lo + LRU_W]

  fillers = [functools.partial(project_lru_section, "xb", V_END),
             functools.partial(project_lru_section, "yb", X_END)]

  def filler():
    if fillers:
      fillers.pop(0)()
    else:
      ffn.run(1)

  lane = lax.broadcasted_iota(jnp.int32, (tq, KV_W), 1)
  is_lo = lane < HEAD_DIM
  for src, dst in ((k, kz_s), (v, vz_s)):
    swapped = pltpu.roll(src, HEAD_DIM, axis=1)
    dst[0, WINDOW:WINDOW + tq, :] = jnp.where(is_lo, src, 0.0).astype(BF16)
    dst[1, WINDOW:WINDOW + tq, :] = jnp.where(is_lo, 0.0, swapped).astype(BF16)
    dst[2, WINDOW:WINDOW + tq, :] = jnp.where(is_lo, swapped, 0.0).astype(BF16)
    dst[3, WINDOW:WINDOW + tq, :] = jnp.where(is_lo, 0.0, src).astype(BF16)

  own_ok = (col >= WINDOW) & (dist >= 0)
  prev_dist = jnp.where(col < WINDOW, dist, 2 * WINDOW)
  lane_q = lax.broadcasted_iota(jnp.int32, (WINDOW, KV_W), 1)

  def operand(ref, j, p):
    g = p // (N_PAIRS // N_KV_HEADS)
    r0 = j * WINDOW
    return jnp.concatenate(
        [ref[2 * g, r0:r0 + 2 * WINDOW, :], ref[2 * g + 1, r0:r0 + 2 * WINDOW, :]], axis=0)

  def scores(j, p):
    r0 = j * WINDOW
    return _dot_t(q_s[r0:r0 + WINDOW, p * KV_W:(p + 1) * KV_W], operand(kz_s, j, p))

  def softmax(s, j, p):
    prev_limit = jnp.where(first_tile, 0, WINDOW) if j == 0 else WINDOW
    valid = own_ok | (prev_dist < prev_limit)
    probs, inv = [], []
    for hh in range(2):
      h = 2 * p + hh
      sink = sinks_ref[h]
      sh = s[:, hh * 2 * WINDOW:(hh + 1) * 2 * WINDOW]
      sh = jnp.where(valid, sh - bias_s[h], NEG)
      m = jnp.maximum(jnp.max(sh, axis=-1, keepdims=True), sink)
      e = jnp.exp(sh - m)
      denom = jnp.sum(e, axis=-1, keepdims=True) + jnp.exp(sink - m)
      probs.append(e.astype(BF16))
      inv.append(1.0 / denom)
    return jnp.concatenate(probs, axis=-1), jnp.where(lane_q < HEAD_DIM, inv[0], inv[1])

  def weighted_values(j, p, probs, inv):
    r0 = j * WINDOW
    o = _dot(probs, operand(vz_s, j, p)) * inv
    attn_s[r0:r0 + WINDOW, p * KV_W:(p + 1) * KV_W] = o.astype(BF16)

  lru_state = {}

  def lru_unit(rc):
    xb, yb = lru_in["xb"], lru_in["yb"]
    if rc == 0:
      xpad_s[SUBLANES:SUBLANES + tq, :] = xb
      lru_state["h"] = hcar_s[...]
    h = lru_state["h"]
    c0 = rc * LRU_ROWS
    xc = convb_ref[...] + convw_ref[CONV_W - 1:CONV_W, :] * xb[c0:c0 + LRU_ROWS, :]
    for kk in range(CONV_W - 1):
      off = c0 + SUBLANES - (CONV_W - 1) + kk
      xc = xc + convw_ref[kk:kk + 1, :] * xpad_s[off:off + LRU_ROWS, :]
    a, b = _lru_coeffs(xc, wa_ref, ba_ref, wi_ref, bi_ref, lam_ref)
    filler()
    a, b = _scan_within_groups(a, b, SUBLANES)
    for gi in range(LRU_ROWS // SUBLANES):
      r0 = gi * SUBLANES
      hg = a[r0:r0 + SUBLANES, :] * h + b[r0:r0 + SUBLANES, :]
      hs_s[c0 + r0:c0 + r0 + SUBLANES, :] = hg
      h = jnp.broadcast_to(hg[SUBLANES - 1:SUBLANES, :], (SUBLANES, LRU_W))
    lru_s[c0:c0 + LRU_ROWS, :] = (
        hs_s[c0:c0 + LRU_ROWS, :] * jax.nn.gelu(yb[c0:c0 + LRU_ROWS, :])).astype(BF16)
    filler()
    lru_state["h"] = h

  problems = [(j, p) for j in range(tq // WINDOW) for p in range(N_PAIRS)]
  s_next = scores(*problems[0])
  for i, (j, p) in enumerate(problems):
    s_cur = s_next
    if i + 1 < len(problems):
      s_next = scores(*problems[i + 1])
    probs, inv = softmax(s_cur, j, p)
    filler()
    weighted_values(j, p, probs, inv)
    if p == N_PAIRS - 1:
      lru_unit(j)

  kz_s[:, 0:WINDOW, :] = kz_s[:, tq:tq + WINDOW, :]
  vz_s[:, 0:WINDOW, :] = vz_s[:, tq:tq + WINDOW, :]
  xb = lru_in["xb"]
  h = lru_state["h"]
  xpad_s[0:SUBLANES, :] = xpad_s[tq:tq + SUBLANES, :]
  hcar_s[...] = h

  @pl.when(last_tile)
  def _():
    nk_ref[0] = k[tq - WINDOW:tq, :]
    nv_ref[0] = v[tq - WINDOW:tq, :]
    nconv_ref[0] = xb[tq - (CONV_W - 1):tq, :]
    nh_ref[0] = h[0:1, :]

  hres = (x + _dot(attn_s[...], wout_ref[0:ATTN_W, :])
          + _dot(lru_s[...], wout_ref[ATTN_W:ATTN_W + LRU_W, :]) + bout_ref[...])
  y_ref[0] = ffn.finish()
  hres_s[slot] = hres
  hn_s[...] = _rmsnorm(hres, gffn_ref[...]).astype(BF16)


def _whole(space):
  return pl.BlockSpec(memory_space=space)


def _prompt_call(x, sinks, weights):
  batch, seq, _ = x.shape
  tq = PROMPT_TILE
  n_t = seq // tq
  n_tiles = batch * n_t
  vm = _whole(pltpu.VMEM)

  def mixer_tile(s):
    return jnp.minimum(s, n_tiles - 1)

  def ffn_tile(s):
    return jnp.maximum(s - 1, 0)

  in_specs = [_whole(pltpu.SMEM),
              pl.BlockSpec((1, tq, D_MODEL), lambda s: (mixer_tile(s) // n_t, mixer_tile(s) % n_t, 0))]
  in_specs += [vm] * len(weights)
  out_shape = (
      jax.ShapeDtypeStruct((batch, seq, D_MODEL), F32),
      jax.ShapeDtypeStruct((batch, WINDOW, KV_W), F32),
      jax.ShapeDtypeStruct((batch, WINDOW, KV_W), F32),
      jax.ShapeDtypeStruct((batch, CONV_W - 1, LRU_W), F32),
      jax.ShapeDtypeStruct((batch, 1, LRU_W), F32),
  )
  state_map = lambda s: (mixer_tile(s) // n_t, 0, 0)
  out_specs = (
      pl.BlockSpec((1, tq, D_MODEL), lambda s: (ffn_tile(s) // n_t, ffn_tile(s) % n_t, 0)),
      pl.BlockSpec((1, WINDOW, KV_W), state_map),
      pl.BlockSpec((1, WINDOW, KV_W), state_map),
      pl.BlockSpec((1, CONV_W - 1, LRU_W), state_map),
      pl.BlockSpec((1, 1, LRU_W), state_map),
  )
  scratch = [
      pltpu.VMEM((tq, ATTN_W), BF16),
      pltpu.VMEM((4, WINDOW + tq, KV_W), BF16),
      pltpu.VMEM((4, WINDOW + tq, KV_W), BF16),
      pltpu.VMEM((SUBLANES + tq, LRU_W), F32),
      pltpu.VMEM((tq, LRU_W), F32),
      pltpu.VMEM((SUBLANES, LRU_W), F32),
      pltpu.VMEM((tq, ATTN_W), BF16),
      pltpu.VMEM((tq, LRU_W), BF16),
      pltpu.VMEM((2, tq, D_MODEL), F32),
      pltpu.VMEM((tq, D_MODEL), BF16),
      pltpu.VMEM((tq, D_FF), BF16),
      pltpu.VMEM((N_HEADS, WINDOW, 2 * WINDOW), F32),
  ]
  return pl.pallas_call(
      functools.partial(_prompt_kernel, tiles_per_seq=n_t),
      grid=(n_tiles + 1,),
      in_specs=in_specs,
      out_specs=out_specs,
      out_shape=out_shape,
      scratch_shapes=scratch,
      compiler_params=pltpu.CompilerParams(
          dimension_semantics=("arbitrary",),
          vmem_limit_bytes=VMEM_LIMIT_BYTES),
      name="prompt_layer",
  )(sinks, x, *weights)


def _sample_kernel(
    sink_ref, x_ref, ck_ref, cv_ref, cpad_ref, h0_ref, gmix_ref, win_ref, bin_ref, convw_ref,
    convb_ref, wa_ref, ba_ref, wi_ref, bi_ref, lam_ref, wout_ref, bout_ref, gffn_ref, wfi_ref,
    wfo_ref, gfin_ref,
    y_ref, nk_ref, nv_ref, xb_ref, hs_ref,
    q_s, kn_s, vn_s, yb_s, attn_s, act_s):
  n_tok = CONV_W
  rows = x_ref.shape[0]
  step = pl.program_id(0)

  @pl.when(step == 0)
  def _():
    u = _rmsnorm(x_ref[...], gmix_ref[...]).astype(BF16)
    q_s[...] = (_dot(u, win_ref[:, 0:Q_END]) + bin_ref[:, 0:Q_END]) * (HEAD_DIM ** -0.5)
    kv = _dot(u, win_ref[:, Q_END:V_END]) + bin_ref[:, Q_END:V_END]
    kn_s[...] = kv[:, 0:KV_W]
    vn_s[...] = kv[:, KV_W:2 * KV_W]
    xb_ref[...] = _dot(u, win_ref[:, V_END:X_END]) + bin_ref[:, V_END:X_END]
    yb_s[...] = _dot(u, win_ref[:, X_END:IN_W]) + bin_ref[:, X_END:IN_W]

  prow = lax.broadcasted_iota(jnp.int32, (PAIR_ROWS, 2 * WINDOW), 0)
  pcol = lax.broadcasted_iota(jnp.int32, (PAIR_ROWS, 2 * WINDOW), 1)
  seq_r = (prow % SUBLANES) // n_tok
  tok_r = prow % n_tok
  dist_c = WINDOW + tok_r - (pcol % WINDOW)
  valid_c = (seq_r == pcol // WINDOW) & (dist_c >= 0) & (dist_c < WINDOW)
  nrow = lax.broadcasted_iota(jnp.int32, (PAIR_ROWS, LANES), 0)
  ncol = lax.broadcasted_iota(jnp.int32, (PAIR_ROWS, LANES), 1)
  dist_n = (nrow % n_tok) - (ncol % n_tok)
  valid_n = (ncol < SUBLANES) & ((nrow % SUBLANES) // n_tok == ncol // n_tok) & (dist_n >= 0)
  slope_c = jnp.zeros((PAIR_ROWS, 2 * WINDOW), F32)
  slope_n = jnp.zeros((PAIR_ROWS, LANES), F32)
  for h in range(N_HEADS):
    slope_c = jnp.where(prow // SUBLANES == h, _alibi_slope(h), slope_c)
    slope_n = jnp.where(nrow // SUBLANES == h, _alibi_slope(h), slope_n)
  bias_c = slope_c * dist_c.astype(F32)
  bias_n = slope_n * dist_n.astype(F32)
  sink = sink_ref[...]
  pad = jnp.zeros((LANES - SUBLANES, KV_W), F32)
  low_half = lax.broadcasted_iota(jnp.int32, (SUBLANES, KV_W), 1) < HEAD_DIM
  first_row = step * (SAMPLE_SEQS * n_tok)
  kept = WINDOW - n_tok

  def scores(c):
    r0 = pl.multiple_of(first_row + c * SUBLANES, SUBLANES)
    blocks = []
    for h in range(N_HEADS):
      grp = q_s[pl.ds(r0, SUBLANES), (h // 2) * KV_W:(h // 2 + 1) * KV_W]
      to_low = h // GROUP == 0
      if (h % 2 == 0) != to_low:
        grp = pltpu.roll(grp, HEAD_DIM, axis=1)
      blocks.append(jnp.where(low_half == to_low, grp, 0.0))
    lhs = jnp.concatenate(blocks, axis=0).astype(BF16)
    kn = kn_s[pl.ds(r0, SUBLANES), :]
    kc = ck_ref[pl.ds(2 * c, 2)].reshape(2 * WINDOW, KV_W).astype(BF16)
    s_c = jnp.where(valid_c, _dot_t(lhs, kc) - bias_c, NEG)
    s_n = jnp.where(
        valid_n, _dot_t(lhs, jnp.concatenate([kn, pad], axis=0).astype(BF16)) - bias_n, NEG)
    return s_c, s_n

  def softmax(s_c, s_n):
    m = jnp.maximum(
        jnp.maximum(jnp.max(s_c, axis=-1, keepdims=True), jnp.max(s_n, axis=-1, keepdims=True)),
        sink)
    e_c = jnp.exp(s_c - m)
    e_n = jnp.exp(s_n - m)
    denom = (jnp.sum(e_c, axis=-1, keepdims=True) + jnp.sum(e_n, axis=-1, keepdims=True)
             + jnp.exp(sink - m))
    return e_c.astype(BF16), e_n.astype(BF16), 1.0 / denom

  def weighted_values(c, e_c, e_n, inv):
    r0 = pl.multiple_of(first_row + c * SUBLANES, SUBLANES)
    vn = vn_s[pl.ds(r0, SUBLANES), :]
    vc = cv_ref[pl.ds(2 * c, 2)].reshape(2 * WINDOW, KV_W).astype(BF16)
    o = (_dot(e_c, vc) + _dot(e_n, jnp.concatenate([vn, pad], axis=0).astype(BF16))) * inv
    for hp in range(N_PAIRS):
      lo = o[2 * hp * SUBLANES:(2 * hp + 1) * SUBLANES, :]
      hi = o[(2 * hp + 1) * SUBLANES:(2 * hp + 2) * SUBLANES, :]
      if (2 * hp) // GROUP == 0:
        hi = pltpu.roll(hi, HEAD_DIM, axis=1)
      else:
        lo = pltpu.roll(lo, HEAD_DIM, axis=1)
      attn_s[pl.ds(r0, SUBLANES), hp * KV_W:(hp + 1) * KV_W] = jnp.where(low_half, lo, hi)

  def shift_caches(c):
    r0 = pl.multiple_of(first_row + c * SUBLANES, SUBLANES)
    for new_ref, cache_ref, out_ref in ((kn_s, ck_ref, nk_ref), (vn_s, cv_ref, nv_ref)):
      out_ref[pl.ds(2 * c, 2), 0:kept, :] = cache_ref[pl.ds(2 * c, 2), n_tok:WINDOW, :]
      out_ref[pl.ds(2 * c, 1), kept:WINDOW, :] = new_ref[pl.ds(r0, n_tok), :].reshape(1, n_tok, KV_W)
      out_ref[pl.ds(2 * c + 1, 1), kept:WINDOW, :] = (
          new_ref[pl.ds(r0 + n_tok, n_tok), :].reshape(1, n_tok, KV_W))

  def pair_group(gi, carry):
    cs = [gi * SAMPLE_PAIRS_PER_ITER + i for i in range(SAMPLE_PAIRS_PER_ITER)]
    raw = [scores(c) for c in cs]
    probs = [softmax(*s) for s in raw]
    for c, pr in zip(cs, probs):
      weighted_values(c, *pr)
      shift_caches(c)
    return carry

  lax.fori_loop(0, SAMPLE_SEQS // 2 // SAMPLE_PAIRS_PER_ITER, pair_group, 0)

  @pl.when(step == pl.num_programs(0) - 1)
  def _():
    x = x_ref[...]
    xb = xb_ref[...]
    tok = lax.broadcasted_iota(jnp.int32, (rows, LRU_W), 0) % n_tok
    cpad = cpad_ref[...]
    xc = convb_ref[...] + convw_ref[CONV_W - 1:CONV_W, :] * xb
    for shift in range(1, CONV_W):
      saved = cpad if shift == CONV_W - 1 else pltpu.roll(cpad, rows - (CONV_W - 1 - shift), axis=0)
      prev = jnp.where(tok >= shift, pltpu.roll(xb, shift, axis=0), saved)
      xc = xc + convw_ref[CONV_W - 1 - shift:CONV_W - shift, :] * prev

    a, b = _lru_coeffs(xc, wa_ref, ba_ref, wi_ref, bi_ref, lam_ref)
    a, b = _scan_within_groups(a, b, n_tok)
    hs = a * h0_ref[...] + b
    hs_ref[...] = hs
    lru = hs * jax.nn.gelu(yb_s[...])

    hres = (x + _dot(attn_s[...].astype(BF16), wout_ref[0:ATTN_W, :])
            + _dot(lru.astype(BF16), wout_ref[ATTN_W:ATTN_W + LRU_W, :]) + bout_ref[...])
    hn = _rmsnorm(hres, gffn_ref[...]).astype(BF16)
    y_ref[...] = _Ffn(hres, hn, wfi_ref, wfo_ref, gfin_ref, act_s).finish()


def _sample_call(x, ck, cv, cpad, h0rep, sink_rows, weights):
  n_rows = x.shape[0]
  n_seqs = ck.shape[0]
  vm = _whole(pltpu.VMEM)
  cache_spec = pl.BlockSpec((SAMPLE_SEQS, WINDOW, KV_W), lambda i: (i, 0, 0))
  in_specs = [vm, vm, cache_spec, cache_spec, vm, vm] + [vm] * len(weights)
  out_shape = (
      jax.ShapeDtypeStruct((n_rows, D_MODEL), F32),
      jax.ShapeDtypeStruct((n_seqs, WINDOW, KV_W), F32),
      jax.ShapeDtypeStruct((n_seqs, WINDOW, KV_W), F32),
      jax.ShapeDtypeStruct((n_rows, LRU_W), F32),
      jax.ShapeDtypeStruct((n_rows, LRU_W), F32),
  )
  out_specs = (vm, cache_spec, cache_spec, vm, vm)
  scratch = [
      pltpu.VMEM((n_rows, ATTN_W), F32),
      pltpu.VMEM((n_rows, KV_W), F32),
      pltpu.VMEM((n_rows, KV_W), F32),
      pltpu.VMEM((n_rows, LRU_W), F32),
      pltpu.VMEM((n_rows, ATTN_W), F32),
      pltpu.VMEM((n_rows, D_FF), BF16),
  ]
  return pl.pallas_call(
      _sample_kernel,
      grid=(n_seqs // SAMPLE_SEQS,),
      in_specs=in_specs,
      out_specs=out_specs,
      out_shape=out_shape,
      scratch_shapes=scratch,
      compiler_params=pltpu.CompilerParams(
          dimension_semantics=("arbitrary",),
          vmem_limit_bytes=VMEM_LIMIT_BYTES),
      name="sample_layer",
  )(sink_rows, x, ck, cv, cpad, h0rep, *weights)


def _block_diag_halves(w):
  per = LRU_BLOCKS // 2
  blocks = w.reshape(2, per, 1, LRU_BW, LRU_BW)
  on_diag = jnp.eye(per, dtype=bool).reshape(1, per, per, 1, 1)
  tiles = jnp.where(on_diag, blocks, 0.0)
  return tiles.transpose(0, 1, 3, 2, 4).reshape(2, per * LRU_BW, per * LRU_BW).astype(BF16)


def kernel(x_prompt, x_sample, cache_k, cache_v, state_conv, state_h, g_mix, w_in, b_in,
           attn_sinks, conv_w, conv_b, w_a, b_a, w_i, b_i, lam, w_out, b_out, g_ffn,
           w_ffn_in, w_ffn_out, g_final):
  depth = g_mix.shape[0]
  assert depth == 1
  batch, seq, _ = x_prompt.shape
  dec_batch, dec_seq, _ = x_sample.shape
  assert seq % PROMPT_TILE == 0 and dec_seq == CONV_W and dec_batch % SAMPLE_SEQS == 0

  row = lambda p: p.reshape(1, -1)
  weights = [row(g_mix[0]), w_in[0].astype(BF16), row(b_in[0]), conv_w[0], row(conv_b[0]),
             _block_diag_halves(w_a[0]), row(b_a[0]), _block_diag_halves(w_i[0]), row(b_i[0]),
             row(lam[0]), w_out[0].astype(BF16), row(b_out[0]), row(g_ffn[0]),
             w_ffn_in[0].astype(BF16), w_ffn_out[0].astype(BF16), row(g_final)]

  y_p, nk_p, nv_p, nconv_p, nh_p = _prompt_call(x_prompt, attn_sinks[0], weights)

  n_rows = dec_batch * dec_seq
  cpad = jnp.pad(state_conv[0], ((0, 0), (0, 1), (0, 0))).reshape(n_rows, LRU_W)
  h0rep = jnp.repeat(state_h[0], dec_seq, axis=0)
  sink_rows = jnp.repeat(attn_sinks[0], SUBLANES).reshape(PAIR_ROWS, 1)
  y_s, nk_s, nv_s, xb_s, hs_s = _sample_call(
      x_sample.reshape(n_rows, D_MODEL), cache_k[0].reshape(dec_batch, WINDOW, KV_W),
      cache_v[0].reshape(dec_batch, WINDOW, KV_W), cpad, h0rep, sink_rows, weights)
  nconv_s = xb_s.reshape(dec_batch, dec_seq, LRU_W)[:, dec_seq - (CONV_W - 1):]
  nh_s = hs_s.reshape(dec_batch, dec_seq, LRU_W)[:, dec_seq - 1]

  pkv = (1, batch, WINDOW, N_KV_HEADS, HEAD_DIM)
  skv = (1, dec_batch, WINDOW, N_KV_HEADS, HEAD_DIM)
  return (y_p, y_s.reshape(dec_batch, dec_seq, D_MODEL),
          nk_p.reshape(pkv), nv_p.reshape(pkv), nconv_p[None], nh_p.reshape(1, batch, LRU_W),
          nk_s.reshape(skv), nv_s.reshape(skv), nconv_s[None], nh_s[None])
```

```python
import functools

import jax
import jax.numpy as jnp
from jax import lax
from jax.experimental import pallas as pl
from jax.experimental.pallas import tpu as pltpu

D_MODEL = 1024
N_HEADS = 8
N_KV_HEADS = 2
HEAD_DIM = 64
GROUP = N_HEADS // N_KV_HEADS
ATTN_W = N_HEADS * HEAD_DIM
KV_W = N_KV_HEADS * HEAD_DIM
WINDOW = 128
LRU_W = D_MODEL // 2
LRU_BLOCKS = 8
LRU_BW = LRU_W // LRU_BLOCKS
CONV_W = 4
LRU_C = 8.0
IN_W = ATTN_W + 2 * KV_W + 2 * LRU_W
Q_END = ATTN_W
K_END = Q_END + KV_W
V_END = K_END + KV_W
X_END = V_END + LRU_W
D_FF = 2816
EPS = 1e-6
NEG = -1e30
LOG2E = 1.4426950408889634

SUBLANES = 8
LANES = 128
VMEM_LIMIT_BYTES = 56 * 1024 * 1024

PROMPT_TILE = 512
FFN_CHUNK = 256
DOWN_N = 512
LRU_ROWS = WINDOW
N_PAIRS = N_HEADS // 2
SAMPLE_SEQS = 16
PAIR_ROWS = N_HEADS * SUBLANES
SAMPLE_PAIRS_PER_ITER = 4

F32 = jnp.float32
BF16 = jnp.bfloat16


def _dot(a, b):
  return jnp.dot(a, b, preferred_element_type=F32)


def _dot_t(a, b):
  return lax.dot_general(a, b, (((1,), (1,)), ((), ())), preferred_element_type=F32)


def _rmsnorm(x, g):
  return x * lax.rsqrt(jnp.mean(x * x, axis=-1, keepdims=True) + EPS) * g


def _alibi_slope(h):
  return 2.0 ** (-8.0 * (h + 1) / N_HEADS)


def _lru_coeffs(xc, wa_ref, ba_ref, wi_ref, bi_ref, lam_ref):
  xcb = xc.astype(BF16)
  half = LRU_W // 2
  lo, hi = xcb[:, :half], xcb[:, half:]
  r = jnp.concatenate([_dot(lo, wa_ref[0]), _dot(hi, wa_ref[1])], axis=-1) + ba_ref[...]
  ig = jnp.concatenate([_dot(lo, wi_ref[0]), _dot(hi, wi_ref[1])], axis=-1) + bi_ref[...]
  r = jax.nn.sigmoid(r)
  ig = jax.nn.sigmoid(ig)
  nlam = -lam_ref[...]
  softplus = jnp.maximum(nlam, 0.0) + jnp.log1p(jnp.exp(-jnp.abs(nlam)))
  log_a = -LRU_C * r * softplus
  a = jnp.exp(log_a)
  gain_sq = -jnp.tanh(log_a) * (a * a + 1.0)
  gain = jnp.where(gain_sq > 0.0, gain_sq * lax.rsqrt(gain_sq), 0.0)
  return a, gain * (ig * xc)


def _scan_within_groups(a, b, period):
  t, w = a.shape
  a3 = a.reshape(t // SUBLANES, SUBLANES, w)
  b3 = b.reshape(t // SUBLANES, SUBLANES, w)
  pos = lax.broadcasted_iota(jnp.int32, (1, SUBLANES, w), 1) % period
  d = 1
  while d < period:
    keep = pos >= d
    a_prev = jnp.where(keep, pltpu.roll(a3, d, axis=1), 1.0)
    b_prev = jnp.where(keep, pltpu.roll(b3, d, axis=1), 0.0)
    b3 = a3 * b_prev + b3
    a3 = a3 * a_prev
    d *= 2
  return a3.reshape(t, w), b3.reshape(t, w)


class _Ffn:
  def __init__(self, hres, hn, wfi_ref, wfo_ref, gfin_ref, act_ref):
    self.hres, self.hn = hres, hn
    self.wfi_ref, self.wfo_ref, self.gfin_ref, self.act_ref = wfi_ref, wfo_ref, gfin_ref, act_ref
    self.n_up = D_FF // FFN_CHUNK
    self.n_units = self.n_up + D_MODEL // DOWN_N
    self.done = 0
    self.pieces = []
    self.sumsq = None

  def run(self, n_units=1):
    for _ in range(min(n_units, self.n_units - self.done)):
      if self.done < self.n_up:
        lo = self.done * FFN_CHUNK
        w = jnp.concatenate([self.wfi_ref[:, lo:lo + FFN_CHUNK],
                             self.wfi_ref[:, D_FF + lo:D_FF + lo + FFN_CHUNK]], axis=-1)
        gate_up = _dot(self.hn, w)
        gate, up = gate_up[:, :FFN_CHUNK], gate_up[:, FFN_CHUNK:]
        self.act_ref[:, lo:lo + FFN_CHUNK] = (jax.nn.silu(gate) * up).astype(BF16)
      else:
        lo = (self.done - self.n_up) * DOWN_N
        piece = (self.hres[:, lo:lo + DOWN_N]
                 + _dot(self.act_ref[...], self.wfo_ref[:, lo:lo + DOWN_N]))
        self.pieces.append(piece)
        sq = jnp.sum(piece * piece, axis=-1, keepdims=True)
        self.sumsq = sq if self.sumsq is None else self.sumsq + sq
      self.done += 1

  def finish(self):
    self.run(self.n_units)
    scale = lax.rsqrt(self.sumsq * (1.0 / D_MODEL) + EPS)
    return jnp.concatenate(self.pieces, axis=-1) * scale * self.gfin_ref[...]


def _prompt_kernel(
    sinks_ref, x_ref, gmix_ref, win_ref, bin_ref, convw_ref, convb_ref, wa_ref, ba_ref,
    wi_ref, bi_ref, lam_ref, wout_ref, bout_ref, gffn_ref, wfi_ref, wfo_ref, gfin_ref,
    y_ref, nk_ref, nv_ref, nconv_ref, nh_ref,
    q_s, kz_s, vz_s, xpad_s, yb_s, hin_s, hcar_s, attn_s, lru_s, hres_s, hn_s, act_s, bias_s, *,
    tiles_per_seq):
  tq = PROMPT_TILE
  step = pl.program_id(0)
  slot = step % 2
  t_idx = step % tiles_per_seq
  first_tile = t_idx == 0
  last_tile = t_idx == tiles_per_seq - 1

  row = lax.broadcasted_iota(jnp.int32, (WINDOW, 2 * WINDOW), 0)
  col = lax.broadcasted_iota(jnp.int32, (WINDOW, 2 * WINDOW), 1)
  dist = WINDOW + row - col

  @pl.when(step == 0)
  def _():
    hres_s[...] = jnp.zeros((2, tq, D_MODEL), F32)
    hn_s[...] = jnp.zeros((tq, D_MODEL), BF16)
    for h in range(N_HEADS):
      bias_s[h] = (_alibi_slope(h) * LOG2E) * dist.astype(F32)

  @pl.when(first_tile)
  def _():
    kz_s[:, 0:WINDOW, :] = jnp.zeros((4, WINDOW, KV_W), BF16)
    vz_s[:, 0:WINDOW, :] = jnp.zeros((4, WINDOW, KV_W), BF16)
    xpad_s[:, 0:SUBLANES, :] = jnp.zeros((LRU_W // LANES, SUBLANES, LANES), F32)
    hcar_s[...] = jnp.zeros((SUBLANES, LRU_W), F32)

  ffn = _Ffn(hres_s[1 - slot], hn_s[...], wfi_ref, wfo_ref, gfin_ref, act_s)
  ffn.run(1)

  x = x_ref[0]
  u = _rmsnorm(x, gmix_ref[...]).astype(BF16)

  q = _dot(u, win_ref[:, 0:Q_END]) + bin_ref[:, 0:Q_END]
  q_s[...] = (q * (HEAD_DIM ** -0.5 * LOG2E)).astype(BF16)
  w_kv = win_ref[:, Q_END:V_END]
  kv = jnp.concatenate([_dot(u[:tq // 2], w_kv), _dot(u[tq // 2:], w_kv)], axis=0)
  kv = kv + bin_ref[:, Q_END:V_END]
  k = kv[:, 0:KV_W]
  v = kv[:, KV_W:2 * KV_W]
  lru_in = {}

  def project_lru_section(name, lo):
    lru_in[name] = _dot(u, win_ref[:, lo:lo + LRU_W]) + bin_ref[:, lo:lo + LRU_W]

  fillers = [functools.partial(project_lru_section, "xb", V_END),
             functools.partial(project_lru_section, "yb", X_END)]

  def filler():
    if fillers:
      fillers.pop(0)()
    elif ffn.done < ffn.n_units - 1:
      ffn.run(1)

  lane = lax.broadcasted_iota(jnp.int32, (tq, KV_W), 1)
  is_lo = lane < HEAD_DIM
  for src, dst in ((k, kz_s), (v, vz_s)):
    swapped = pltpu.roll(src, HEAD_DIM, axis=1)
    dst[0, WINDOW:WINDOW + tq, :] = jnp.where(is_lo, src, 0.0).astype(BF16)
    dst[1, WINDOW:WINDOW + tq, :] = jnp.where(is_lo, 0.0, swapped).astype(BF16)
    dst[2, WINDOW:WINDOW + tq, :] = jnp.where(is_lo, swapped, 0.0).astype(BF16)
    dst[3, WINDOW:WINDOW + tq, :] = jnp.where(is_lo, 0.0, src).astype(BF16)

  own_ok = (col >= WINDOW) & (dist >= 0)
  prev_dist = jnp.where(col < WINDOW, dist, 2 * WINDOW)
  lane_q = lax.broadcasted_iota(jnp.int32, (WINDOW, KV_W), 1)

  def operand(ref, j, p):
    g = p // (N_PAIRS // N_KV_HEADS)
    r0 = j * WINDOW
    return jnp.concatenate(
        [ref[2 * g, r0:r0 + 2 * WINDOW, :], ref[2 * g + 1, r0:r0 + 2 * WINDOW, :]], axis=0)

  def scores(j, p):
    r0 = j * WINDOW
    return _dot_t(q_s[r0:r0 + WINDOW, p * KV_W:(p + 1) * KV_W], operand(kz_s, j, p))

  def softmax(s, j, p):
    prev_limit = jnp.where(first_tile, 0, WINDOW) if j == 0 else WINDOW
    valid = own_ok | (prev_dist < prev_limit)
    probs, inv = [], []
    for hh in range(2):
      h = 2 * p + hh
      sink = sinks_ref[h] * LOG2E
      sh = s[:, hh * 2 * WINDOW:(hh + 1) * 2 * WINDOW]
      sh = jnp.where(valid, sh - bias_s[h], NEG)
      m = jnp.maximum(jnp.max(sh, axis=-1, keepdims=True), sink)
      e = jnp.exp2(sh - m)
      denom = jnp.sum(e, axis=-1, keepdims=True) + jnp.exp2(sink - m)
      probs.append(e.astype(BF16))
      inv.append(1.0 / denom)
    return jnp.concatenate(probs, axis=-1), jnp.where(lane_q < HEAD_DIM, inv[0], inv[1])

  def weighted_values(j, p, probs, inv):
    r0 = j * WINDOW
    o = _dot(probs, operand(vz_s, j, p)) * inv
    attn_s[r0:r0 + WINDOW, p * KV_W:(p + 1) * KV_W] = o.astype(BF16)

  lru_state = {}
  n_groups = LRU_ROWS // SUBLANES

  lane_groups = LRU_W // LANES

  def put_rows(ref, first_row, val):
    for g in range(lane_groups):
      ref[g, first_row:first_row + val.shape[0], :] = val[:, g * LANES:(g + 1) * LANES]

  def phase_view(ref, first_row):
    return jnp.concatenate(
        [ref[g, pl.ds(first_row, n_groups, stride=SUBLANES), :] for g in range(lane_groups)], axis=-1)

  def phase_store(ref, first_row, val):
    for g in range(lane_groups):
      ref[g, pl.ds(first_row, n_groups, stride=SUBLANES), :] = val[:, g * LANES:(g + 1) * LANES]

  def lru_conv(rc):
    if rc == 0:
      put_rows(xpad_s, SUBLANES, lru_in["xb"])
      lru_state["h"] = hcar_s[0:1, :]
    c0 = SUBLANES + rc * LRU_ROWS
    shifted = {d: phase_view(xpad_s, c0 + d) for d in range(1 - CONV_W, SUBLANES)}
    phases = []
    for r in range(SUBLANES):
      xc = convb_ref[...] + convw_ref[CONV_W - 1:CONV_W, :] * shifted[r]
      for kk in range(CONV_W - 1):
        xc = xc + convw_ref[kk:kk + 1, :] * shifted[r - (CONV_W - 1) + kk]
      phases.append(xc)
    lru_state["xc"] = jnp.concatenate(phases, axis=0)

  def lru_recurrence(rc):
    if rc == 0:
      put_rows(yb_s, 0, lru_in["yb"])
    c0 = rc * LRU_ROWS
    a, b = _lru_coeffs(lru_state["xc"], wa_ref, ba_ref, wi_ref, bi_ref, lam_ref)
    phase = lambda t, r: t[r * n_groups:(r + 1) * n_groups, :]
    a_cum, b_cum = [phase(a, 0)], [phase(b, 0)]
    for r in range(1, SUBLANES):
      b_cum.append(phase(a, r) * b_cum[-1] + phase(b, r))
      a_cum.append(phase(a, r) * a_cum[-1])
    h = lru_state["h"]
    for gi in range(n_groups):
      hin_s[gi:gi + 1, :] = h
      h = a_cum[-1][gi:gi + 1, :] * h + b_cum[-1][gi:gi + 1, :]
    lru_state["h"] = h
    h_in = hin_s[...]
    for r in range(SUBLANES):
      hs = a_cum[r] * h_in + b_cum[r]
      gate = jax.nn.gelu(phase_view(yb_s, c0 + r))
      phase_store(lru_s, c0 + r, hs * gate)

  problems = [(j, p) for j in range(tq // WINDOW) for p in range(N_PAIRS)]
  raw = {i: scores(*problems[i]) for i in range(2)}
  ready = {}
  for i, (j, p) in enumerate(problems):
    if i + 2 < len(problems):
      raw[i + 2] = scores(*problems[i + 2])
    ready[i] = softmax(raw.pop(i), j, p)
    filler()
    if i % N_PAIRS == 1:
      lru_recurrence(i // N_PAIRS)
    if i >= 1:
      weighted_values(*problems[i - 1], *ready.pop(i - 1))
    if i % N_PAIRS == 0:
      lru_conv(i // N_PAIRS)
  weighted_values(*problems[-1], *ready.pop(len(problems) - 1))

  kz_s[:, 0:WINDOW, :] = kz_s[:, tq:tq + WINDOW, :]
  vz_s[:, 0:WINDOW, :] = vz_s[:, tq:tq + WINDOW, :]
  xb = lru_in["xb"]
  h = lru_state["h"]
  xpad_s[:, 0:SUBLANES, :] = xpad_s[:, tq:tq + SUBLANES, :]
  hcar_s[...] = jnp.broadcast_to(h, (SUBLANES, LRU_W))

  @pl.when(last_tile)
  def _():
    nk_ref[0] = k[tq - WINDOW:tq, :]
    nv_ref[0] = v[tq - WINDOW:tq, :]
    nconv_ref[0] = xb[tq - (CONV_W - 1):tq, :]
    nh_ref[0] = h

  hres = (x + _dot(attn_s[...], wout_ref[0:ATTN_W, :])
          + _dot(jnp.concatenate([lru_s[g] for g in range(lane_groups)], axis=-1).astype(BF16),
                 wout_ref[ATTN_W:ATTN_W + LRU_W, :]) + bout_ref[...])
  hres_s[slot] = hres
  hn_s[...] = _rmsnorm(hres, gffn_ref[...]).astype(BF16)
  y_ref[0] = ffn.finish()


def _whole(space):
  return pl.BlockSpec(memory_space=space)


def _prompt_call(x, sinks, weights):
  batch, seq, _ = x.shape
  tq = PROMPT_TILE
  n_t = seq // tq
  n_tiles = batch * n_t
  vm = _whole(pltpu.VMEM)

  def mixer_tile(s):
    return jnp.minimum(s, n_tiles - 1)

  def ffn_tile(s):
    return jnp.maximum(s - 1, 0)

  in_specs = [_whole(pltpu.SMEM),
              pl.BlockSpec((1, tq, D_MODEL), lambda s: (mixer_tile(s) // n_t, mixer_tile(s) % n_t, 0))]
  in_specs += [vm] * len(weights)
  out_shape = (
      jax.ShapeDtypeStruct((batch, seq, D_MODEL), F32),
      jax.ShapeDtypeStruct((batch, WINDOW, KV_W), F32),
      jax.ShapeDtypeStruct((batch, WINDOW, KV_W), F32),
      jax.ShapeDtypeStruct((batch, CONV_W - 1, LRU_W), F32),
      jax.ShapeDtypeStruct((batch, 1, LRU_W), F32),
  )
  state_map = lambda s: (mixer_tile(s) // n_t, 0, 0)
  out_specs = (
      pl.BlockSpec((1, tq, D_MODEL), lambda s: (ffn_tile(s) // n_t, ffn_tile(s) % n_t, 0)),
      pl.BlockSpec((1, WINDOW, KV_W), state_map),
      pl.BlockSpec((1, WINDOW, KV_W), state_map),
      pl.BlockSpec((1, CONV_W - 1, LRU_W), state_map),
      pl.BlockSpec((1, 1, LRU_W), state_map),
  )
  scratch = [
      pltpu.VMEM((tq, ATTN_W), BF16),
      pltpu.VMEM((4, WINDOW + tq, KV_W), BF16),
      pltpu.VMEM((4, WINDOW + tq, KV_W), BF16),
      pltpu.VMEM((LRU_W // LANES, SUBLANES + tq, LANES), F32),
      pltpu.VMEM((LRU_W // LANES, tq, LANES), F32),
      pltpu.VMEM((LRU_ROWS // SUBLANES, LRU_W), F32),
      pltpu.VMEM((SUBLANES, LRU_W), F32),
      pltpu.VMEM((tq, ATTN_W), BF16),
      pltpu.VMEM((LRU_W // LANES, tq, LANES), F32),
      pltpu.VMEM((2, tq, D_MODEL), F32),
      pltpu.VMEM((tq, D_MODEL), BF16),
      pltpu.VMEM((tq, D_FF), BF16),
      pltpu.VMEM((N_HEADS, WINDOW, 2 * WINDOW), F32),
  ]
  return pl.pallas_call(
      functools.partial(_prompt_kernel, tiles_per_seq=n_t),
      grid=(n_tiles + 1,),
      in_specs=in_specs,
      out_specs=out_specs,
      out_shape=out_shape,
      scratch_shapes=scratch,
      compiler_params=pltpu.CompilerParams(
          dimension_semantics=("arbitrary",),
          vmem_limit_bytes=VMEM_LIMIT_BYTES),
      name="prompt_layer",
  )(sinks, x, *weights)


def _sample_kernel(
    sink_ref, x_ref, ck_ref, cv_ref, cpad_ref, h0_ref, gmix_ref, win_ref, bin_ref, convw_ref,
    convb_ref, wa_ref, ba_ref, wi_ref, bi_ref, lam_ref, wout_ref, bout_ref, gffn_ref, wfi_ref,
    wfo_ref, gfin_ref,
    y_ref, nk_ref, nv_ref, xb_ref, hs_ref,
    q_s, kn_s, vn_s, yb_s, attn_s, act_s):
  n_tok = CONV_W
  rows = x_ref.shape[0]
  step = pl.program_id(0)

  @pl.when(step == 0)
  def _():
    u = _rmsnorm(x_ref[...], gmix_ref[...]).astype(BF16)
    q_s[...] = (_dot(u, win_ref[:, 0:Q_END]) + bin_ref[:, 0:Q_END]) * (HEAD_DIM ** -0.5)
    kv = _dot(u, win_ref[:, Q_END:V_END]) + bin_ref[:, Q_END:V_END]
    kn_s[...] = kv[:, 0:KV_W]
    vn_s[...] = kv[:, KV_W:2 * KV_W]
    xb_ref[...] = _dot(u, win_ref[:, V_END:X_END]) + bin_ref[:, V_END:X_END]
    yb_s[...] = _dot(u, win_ref[:, X_END:IN_W]) + bin_ref[:, X_END:IN_W]

  prow = lax.broadcasted_iota(jnp.int32, (PAIR_ROWS, 2 * WINDOW), 0)
  pcol = lax.broadcasted_iota(jnp.int32, (PAIR_ROWS, 2 * WINDOW), 1)
  seq_r = (prow % SUBLANES) // n_tok
  tok_r = prow % n_tok
  dist_c = WINDOW + tok_r - (pcol % WINDOW)
  valid_c = (seq_r == pcol // WINDOW) & (dist_c >= 0) & (dist_c < WINDOW)
  nrow = lax.broadcasted_iota(jnp.int32, (PAIR_ROWS, LANES), 0)
  ncol = lax.broadcasted_iota(jnp.int32, (PAIR_ROWS, LANES), 1)
  dist_n = (nrow % n_tok) - (ncol % n_tok)
  valid_n = (ncol < SUBLANES) & ((nrow % SUBLANES) // n_tok == ncol // n_tok) & (dist_n >= 0)
  slope_c = jnp.zeros((PAIR_ROWS, 2 * WINDOW), F32)
  slope_n = jnp.zeros((PAIR_ROWS, LANES), F32)
  for h in range(N_HEADS):
    slope_c = jnp.where(prow // SUBLANES == h, _alibi_slope(h), slope_c)
    slope_n = jnp.where(nrow // SUBLANES == h, _alibi_slope(h), slope_n)
  bias_c = slope_c * dist_c.astype(F32)
  bias_n = slope_n * dist_n.astype(F32)
  sink = sink_ref[...]
  pad = jnp.zeros((LANES - SUBLANES, KV_W), F32)
  low_half = lax.broadcasted_iota(jnp.int32, (SUBLANES, KV_W), 1) < HEAD_DIM
  first_row = step * (SAMPLE_SEQS * n_tok)
  kept = WINDOW - n_tok

  def scores(c):
    r0 = pl.multiple_of(first_row + c * SUBLANES, SUBLANES)
    blocks = []
    for h in range(N_HEADS):
      grp = q_s[pl.ds(r0, SUBLANES), (h // 2) * KV_W:(h // 2 + 1) * KV_W]
      to_low = h // GROUP == 0
      if (h % 2 == 0) != to_low:
        grp = pltpu.roll(grp, HEAD_DIM, axis=1)
      blocks.append(jnp.where(low_half == to_low, grp, 0.0))
    lhs = jnp.concatenate(blocks, axis=0).astype(BF16)
    kn = kn_s[pl.ds(r0, SUBLANES), :]
    kc = ck_ref[pl.ds(2 * c, 2)].reshape(2 * WINDOW, KV_W).astype(BF16)
    s_c = jnp.where(valid_c, _dot_t(lhs, kc) - bias_c, NEG)
    s_n = jnp.where(
        valid_n, _dot_t(lhs, jnp.concatenate([kn, pad], axis=0).astype(BF16)) - bias_n, NEG)
    return s_c, s_n

  def softmax(s_c, s_n):
    m = jnp.maximum(
        jnp.maximum(jnp.max(s_c, axis=-1, keepdims=True), jnp.max(s_n, axis=-1, keepdims=True)),
        sink)
    e_c = jnp.exp(s_c - m)
    e_n = jnp.exp(s_n - m)
    denom = (jnp.sum(e_c, axis=-1, keepdims=True) + jnp.sum(e_n, axis=-1, keepdims=True)
             + jnp.exp(sink - m))
    return e_c.astype(BF16), e_n.astype(BF16), 1.0 / denom

  def weighted_values(c, e_c, e_n, inv):
    r0 = pl.multiple_of(first_row + c * SUBLANES, SUBLANES)
    vn = vn_s[pl.ds(r0, SUBLANES), :]
    vc = cv_ref[pl.ds(2 * c, 2)].reshape(2 * WINDOW, KV_W).astype(BF16)
    o = (_dot(e_c, vc) + _dot(e_n, jnp.concatenate([vn, pad], axis=0).astype(BF16))) * inv
    for hp in range(N_PAIRS):
      lo = o[2 * hp * SUBLANES:(2 * hp + 1) * SUBLANES, :]
      hi = o[(2 * hp + 1) * SUBLANES:(2 * hp + 2) * SUBLANES, :]
      if (2 * hp) // GROUP == 0:
        hi = pltpu.roll(hi, HEAD_DIM, axis=1)
      else:
        lo = pltpu.roll(lo, HEAD_DIM, axis=1)
      attn_s[pl.ds(r0, SUBLANES), hp * KV_W:(hp + 1) * KV_W] = jnp.where(low_half, lo, hi)

  def shift_caches(c):
    r0 = pl.multiple_of(first_row + c * SUBLANES, SUBLANES)
    for new_ref, cache_ref, out_ref in ((kn_s, ck_ref, nk_ref), (vn_s, cv_ref, nv_ref)):
      out_ref[pl.ds(2 * c, 2), 0:kept, :] = cache_ref[pl.ds(2 * c, 2), n_tok:WINDOW, :]
      out_ref[pl.ds(2 * c, 1), kept:WINDOW, :] = new_ref[pl.ds(r0, n_tok), :].reshape(1, n_tok, KV_W)
      out_ref[pl.ds(2 * c + 1, 1), kept:WINDOW, :] = (
          new_ref[pl.ds(r0 + n_tok, n_tok), :].reshape(1, n_tok, KV_W))

  def pair_group(gi, carry):
    cs = [gi * SAMPLE_PAIRS_PER_ITER + i for i in range(SAMPLE_PAIRS_PER_ITER)]
    raw = [scores(c) for c in cs]
    probs = [softmax(*s) for s in raw]
    for c, pr in zip(cs, probs):
      weighted_values(c, *pr)
      shift_caches(c)
    return carry

  lax.fori_loop(0, SAMPLE_SEQS // 2 // SAMPLE_PAIRS_PER_ITER, pair_group, 0)

  @pl.when(step == pl.num_programs(0) - 1)
  def _():
    x = x_ref[...]
    xb = xb_ref[...]
    tok = lax.broadcasted_iota(jnp.int32, (rows, LRU_W), 0) % n_tok
    cpad = cpad_ref[...]
    xc = convb_ref[...] + convw_ref[CONV_W - 1:CONV_W, :] * xb
    for shift in range(1, CONV_W):
      saved = cpad if shift == CONV_W - 1 else pltpu.roll(cpad, rows - (CONV_W - 1 - shift), axis=0)
      prev = jnp.where(tok >= shift, pltpu.roll(xb, shift, axis=0), saved)
      xc = xc + convw_ref[CONV_W - 1 - shift:CONV_W - shift, :] * prev

    a, b = _lru_coeffs(xc, wa_ref, ba_ref, wi_ref, bi_ref, lam_ref)
    a, b = _scan_within_groups(a, b, n_tok)
    hs = a * h0_ref[...] + b
    hs_ref[...] = hs
    lru = hs * jax.nn.gelu(yb_s[...])

    hres = (x + _dot(attn_s[...].astype(BF16), wout_ref[0:ATTN_W, :])
            + _dot(lru.astype(BF16), wout_ref[ATTN_W:ATTN_W + LRU_W, :]) + bout_ref[...])
    hn = _rmsnorm(hres, gffn_ref[...]).astype(BF16)
    y_ref[...] = _Ffn(hres, hn, wfi_ref, wfo_ref, gfin_ref, act_s).finish()


def _sample_call(x, ck, cv, cpad, h0rep, sink_rows, weights):
  n_rows = x.shape[0]
  n_seqs = ck.shape[0]
  vm = _whole(pltpu.VMEM)
  cache_spec = pl.BlockSpec((SAMPLE_SEQS, WINDOW, KV_W), lambda i: (i, 0, 0))
  in_specs = [vm, vm, cache_spec, cache_spec, vm, vm] + [vm] * len(weights)
  out_shape = (
      jax.ShapeDtypeStruct((n_rows, D_MODEL), F32),
      jax.ShapeDtypeStruct((n_seqs, WINDOW, KV_W), F32),
      jax.ShapeDtypeStruct((n_seqs, WINDOW, KV_W), F32),
      jax.ShapeDtypeStruct((n_rows, LRU_W), F32),
      jax.ShapeDtypeStruct((n_rows, LRU_W), F32),
  )
  out_specs = (vm, cache_spec, cache_spec, vm, vm)
  scratch = [
      pltpu.VMEM((n_rows, ATTN_W), F32),
      pltpu.VMEM((n_rows, KV_W), F32),
      pltpu.VMEM((n_rows, KV_W), F32),
      pltpu.VMEM((n_rows, LRU_W), F32),
      pltpu.VMEM((n_rows, ATTN_W), F32),
      pltpu.VMEM((n_rows, D_FF), BF16),
  ]
  return pl.pallas_call(
      _sample_kernel,
      grid=(n_seqs // SAMPLE_SEQS,),
      in_specs=in_specs,
      out_specs=out_specs,
      out_shape=out_shape,
      scratch_shapes=scratch,
      compiler_params=pltpu.CompilerParams(
          dimension_semantics=("arbitrary",),
          vmem_limit_bytes=VMEM_LIMIT_BYTES),
      name="sample_layer",
  )(sink_rows, x, ck, cv, cpad, h0rep, *weights)


def _block_diag_halves(w):
  per = LRU_BLOCKS // 2
  blocks = w.reshape(2, per, 1, LRU_BW, LRU_BW)
  on_diag = jnp.eye(per, dtype=bool).reshape(1, per, per, 1, 1)
  tiles = jnp.where(on_diag, blocks, 0.0)
  return tiles.transpose(0, 1, 3, 2, 4).reshape(2, per * LRU_BW, per * LRU_BW).astype(BF16)


def kernel(x_prompt, x_sample, cache_k, cache_v, state_conv, state_h, g_mix, w_in, b_in,
           attn_sinks, conv_w, conv_b, w_a, b_a, w_i, b_i, lam, w_out, b_out, g_ffn,
           w_ffn_in, w_ffn_out, g_final):
  depth = g_mix.shape[0]
  assert depth == 1
  batch, seq, _ = x_prompt.shape
  dec_batch, dec_seq, _ = x_sample.shape
  assert seq % PROMPT_TILE == 0 and dec_seq == CONV_W and dec_batch % SAMPLE_SEQS == 0

  row = lambda p: p.reshape(1, -1)
  weights = [row(g_mix[0]), w_in[0].astype(BF16), row(b_in[0]), conv_w[0], row(conv_b[0]),
             _block_diag_halves(w_a[0]), row(b_a[0]), _block_diag_halves(w_i[0]), row(b_i[0]),
             row(lam[0]), w_out[0].astype(BF16), row(b_out[0]), row(g_ffn[0]),
             w_ffn_in[0].astype(BF16), w_ffn_out[0].astype(BF16), row(g_final)]

  y_p, nk_p, nv_p, nconv_p, nh_p = _prompt_call(x_prompt, attn_sinks[0], weights)

  n_rows = dec_batch * dec_seq
  cpad = jnp.pad(state_conv[0], ((0, 0), (0, 1), (0, 0))).reshape(n_rows, LRU_W)
  h0rep = jnp.repeat(state_h[0], dec_seq, axis=0)
  sink_rows = jnp.repeat(attn_sinks[0], SUBLANES).reshape(PAIR_ROWS, 1)
  y_s, nk_s, nv_s, xb_s, hs_s = _sample_call(
      x_sample.reshape(n_rows, D_MODEL), cache_k[0].reshape(dec_batch, WINDOW, KV_W),
      cache_v[0].reshape(dec_batch, WINDOW, KV_W), cpad, h0rep, sink_rows, weights)
  nconv_s = xb_s.reshape(dec_batch, dec_seq, LRU_W)[:, dec_seq - (CONV_W - 1):]
  nh_s = hs_s.reshape(dec_batch, dec_seq, LRU_W)[:, dec_seq - 1]

  pkv = (1, batch, WINDOW, N_KV_HEADS, HEAD_DIM)
  skv = (1, dec_batch, WINDOW, N_KV_HEADS, HEAD_DIM)
  return (y_p, y_s.reshape(dec_batch, dec_seq, D_MODEL),
          nk_p.reshape(pkv), nv_p.reshape(pkv), nconv_p[None], nh_p.reshape(1, batch, LRU_W),
          nk_s.reshape(skv), nv_s.reshape(skv), nconv_s[None], nh_s[None])
```

```python
import functools

import jax
import jax.numpy as jnp
from jax import lax
from jax.experimental import pallas as pl
from jax.experimental.pallas import tpu as pltpu

D_MODEL = 1024
N_HEADS = 8
N_KV_HEADS = 2
HEAD_DIM = 64
GROUP = N_HEADS // N_KV_HEADS
ATTN_W = N_HEADS * HEAD_DIM
KV_W = N_KV_HEADS * HEAD_DIM
WINDOW = 128
LRU_W = D_MODEL // 2
LRU_BLOCKS = 8
LRU_BW = LRU_W // LRU_BLOCKS
CONV_W = 4
LRU_C = 8.0
IN_W = ATTN_W + 2 * KV_W + 2 * LRU_W
Q_END = ATTN_W
K_END = Q_END + KV_W
V_END = K_END + KV_W
X_END = V_END + LRU_W
D_FF = 2816
EPS = 1e-6
NEG = -1e30
LOG2E = 1.4426950408889634

SUBLANES = 8
LANES = 128
VMEM_LIMIT_BYTES = 56 * 1024 * 1024

PROMPT_TILE = 512
FFN_CHUNK = 256
DOWN_ROWS = 256
LRU_ROWS = WINDOW
N_PAIRS = N_HEADS // 2
SAMPLE_SEQS = 16
PAIR_ROWS = N_HEADS * SUBLANES
SAMPLE_PAIRS_PER_ITER = 8

F32 = jnp.float32
BF16 = jnp.bfloat16


def _dot(a, b):
  return jnp.dot(a, b, preferred_element_type=F32)


def _dot_t(a, b):
  return lax.dot_general(a, b, (((1,), (1,)), ((), ())), preferred_element_type=F32)


def _rmsnorm(x, g):
  return x * lax.rsqrt(jnp.mean(x * x, axis=-1, keepdims=True) + EPS) * g


def _alibi_slope(h):
  return 2.0 ** (-8.0 * (h + 1) / N_HEADS)


def _lru_coeffs(xc, wa_ref, ba_ref, wi_ref, bi_ref, lam_ref):
  xcb = xc.astype(BF16)
  half = LRU_W // 2
  lo, hi = xcb[:, :half], xcb[:, half:]
  r = jnp.concatenate([_dot(lo, wa_ref[0]), _dot(hi, wa_ref[1])], axis=-1) + ba_ref[...]
  ig = jnp.concatenate([_dot(lo, wi_ref[0]), _dot(hi, wi_ref[1])], axis=-1) + bi_ref[...]
  r = jax.nn.sigmoid(r)
  ig = jax.nn.sigmoid(ig)
  nlam = -lam_ref[...]
  softplus = jnp.maximum(nlam, 0.0) + jnp.log1p(jnp.exp(-jnp.abs(nlam)))
  log_a = r * (-LRU_C * softplus)
  a = jnp.exp(log_a)
  gain_sq = -jnp.tanh(log_a) * (a * a + 1.0)
  gain = jnp.where(gain_sq > 0.0, gain_sq * lax.rsqrt(gain_sq), 0.0)
  return a, gain * (ig * xc)


def _scan_within_groups(a, b, period):
  t, w = a.shape
  a3 = a.reshape(t // SUBLANES, SUBLANES, w)
  b3 = b.reshape(t // SUBLANES, SUBLANES, w)
  pos = lax.broadcasted_iota(jnp.int32, (1, SUBLANES, w), 1) % period
  d = 1
  while d < period:
    keep = pos >= d
    a_prev = jnp.where(keep, pltpu.roll(a3, d, axis=1), 1.0)
    b_prev = jnp.where(keep, pltpu.roll(b3, d, axis=1), 0.0)
    b3 = a3 * b_prev + b3
    a3 = a3 * a_prev
    d *= 2
  return a3.reshape(t, w), b3.reshape(t, w)


class _Ffn:
  def __init__(self, hres, hn, wfi_ref, wfo_ref, gfin_ref, act_ref, write):
    self.hres, self.hn, self.write = hres, hn, write
    self.wfi_ref, self.wfo_ref, self.gfin_ref, self.act_ref = wfi_ref, wfo_ref, gfin_ref, act_ref
    self.n_up = D_FF // FFN_CHUNK
    self.down_rows = min(DOWN_ROWS, hres.shape[0])
    self.n_units = self.n_up + hres.shape[0] // self.down_rows
    self.done = 0

  def run(self, n_units=1):
    for _ in range(min(n_units, self.n_units - self.done)):
      if self.done < self.n_up:
        lo = self.done * FFN_CHUNK
        w = jnp.concatenate([self.wfi_ref[:, lo:lo + FFN_CHUNK],
                             self.wfi_ref[:, D_FF + lo:D_FF + lo + FFN_CHUNK]], axis=-1)
        gate_up = _dot(self.hn, w)
        gate, up = gate_up[:, :FFN_CHUNK], gate_up[:, FFN_CHUNK:]
        self.act_ref[:, lo:lo + FFN_CHUNK] = (jax.nn.silu(gate) * up).astype(BF16)
      else:
        r0 = (self.done - self.n_up) * self.down_rows
        r1 = r0 + self.down_rows
        out = self.hres[r0:r1, :] + _dot(self.act_ref[r0:r1, :], self.wfo_ref[...])
        self.write(r0, _rmsnorm(out, self.gfin_ref[...]))
      self.done += 1

  def finish(self):
    self.run(self.n_units)


def _prompt_kernel(*refs, tiles_per_seq):
  step = pl.program_id(0)
  last_step = pl.num_programs(0) - 1
  body = functools.partial(_prompt_step, *refs, tiles_per_seq=tiles_per_seq)
  pl.when(step == 0)(functools.partial(body, run_mixer=True, run_ffn=False))
  pl.when(jnp.logical_and(step > 0, step < last_step))(
      functools.partial(body, run_mixer=True, run_ffn=True))
  pl.when(step == last_step)(functools.partial(body, run_mixer=False, run_ffn=True))


def _prompt_step(
    sinks_ref, x_ref, gmix_ref, win_ref, bin_ref, convw_ref, convb_ref, wa_ref, ba_ref,
    wi_ref, bi_ref, lam_ref, wout_ref, bout_ref, gffn_ref, wfi_ref, wfo_ref, gfin_ref,
    y_ref, nk_ref, nv_ref, nconv_ref, nh_ref,
    q_s, kz_s, vz_s, xpad_s, yb_s, hin_s, hcar_s, attn_s, lru_s, hres_s, hn_s, act_s, bias_s, *,
    tiles_per_seq, run_mixer, run_ffn):
  tq = PROMPT_TILE
  step = pl.program_id(0)
  slot = step % 2
  t_idx = step % tiles_per_seq
  first_tile = t_idx == 0
  last_tile = t_idx == tiles_per_seq - 1

  row = lax.broadcasted_iota(jnp.int32, (WINDOW, 2 * WINDOW), 0)
  col = lax.broadcasted_iota(jnp.int32, (WINDOW, 2 * WINDOW), 1)
  dist = WINDOW + row - col

  @pl.when(step == 0)
  def _():
    for h in range(N_HEADS):
      bias_s[h] = (_alibi_slope(h) * LOG2E) * dist.astype(F32)

  @pl.when(first_tile)
  def _():
    kz_s[:, 0:WINDOW, :] = jnp.zeros((4, WINDOW, KV_W), BF16)
    vz_s[:, 0:WINDOW, :] = jnp.zeros((4, WINDOW, KV_W), BF16)
    xpad_s[:, 0:SUBLANES, :] = jnp.zeros((LRU_W // LANES, SUBLANES, LANES), F32)
    hcar_s[...] = jnp.zeros((SUBLANES, LRU_W), F32)

  def write_output(first_row, rows):
    y_ref[0, first_row:first_row + rows.shape[0], :] = rows

  ffn = None
  if run_ffn:
    ffn = _Ffn(hres_s[1 - slot], hn_s[...], wfi_ref, wfo_ref, gfin_ref, act_s, write_output)
    ffn.run(1)
  if not run_mixer:
    ffn.finish()
    return

  x = x_ref[0]
  u = _rmsnorm(x, gmix_ref[...]).astype(BF16)

  q = _dot(u, win_ref[:, 0:Q_END]) + bin_ref[:, 0:Q_END]
  q_s[...] = (q * (HEAD_DIM ** -0.5 * LOG2E)).astype(BF16)
  w_kv = win_ref[:, Q_END:V_END]
  kv = jnp.concatenate([_dot(u[:tq // 2], w_kv), _dot(u[tq // 2:], w_kv)], axis=0)
  kv = kv + bin_ref[:, Q_END:V_END]
  k = kv[:, 0:KV_W]
  v = kv[:, KV_W:2 * KV_W]
  lru_in = {}

  def project_lru_section(name, lo):
    lru_in[name] = _dot(u, win_ref[:, lo:lo + LRU_W]) + bin_ref[:, lo:lo + LRU_W]

  project_lru_section("xb", V_END)

  half = tq // 2
  out_partial = {}

  def project_attention(rows):
    out_partial[rows] = _dot(attn_s[rows * half:(rows + 1) * half, :], wout_ref[0:ATTN_W, :])

  def finish_mixer(rows):
    r0 = rows * half
    lru = jnp.concatenate([lru_s[g, r0:r0 + half, :] for g in range(LRU_W // LANES)], axis=-1)
    hres = (x[r0:r0 + half, :] + out_partial[rows]
            + _dot(lru.astype(BF16), wout_ref[ATTN_W:ATTN_W + LRU_W, :]) + bout_ref[...])
    hres_s[slot, r0:r0 + half, :] = hres
    hn_s[r0:r0 + half, :] = _rmsnorm(hres, gffn_ref[...]).astype(BF16)

  n_problems = (tq // WINDOW) * N_PAIRS
  fixed_fillers = {0: functools.partial(project_lru_section, "yb", X_END),
                   n_problems - 3: functools.partial(project_attention, 0),
                   n_problems - 2: functools.partial(finish_mixer, 0)}

  def filler(i):
    if i in fixed_fillers:
      fixed_fillers[i]()
    elif run_ffn:
      ffn.run(1)

  lane = lax.broadcasted_iota(jnp.int32, (tq, KV_W), 1)
  is_lo = lane < HEAD_DIM
  for src, dst in ((k, kz_s), (v, vz_s)):
    swapped = pltpu.roll(src, HEAD_DIM, axis=1)
    dst[0, WINDOW:WINDOW + tq, :] = jnp.where(is_lo, src, 0.0).astype(BF16)
    dst[1, WINDOW:WINDOW + tq, :] = jnp.where(is_lo, 0.0, swapped).astype(BF16)
    dst[2, WINDOW:WINDOW + tq, :] = jnp.where(is_lo, swapped, 0.0).astype(BF16)
    dst[3, WINDOW:WINDOW + tq, :] = jnp.where(is_lo, 0.0, src).astype(BF16)

  own_ok = (col >= WINDOW) & (dist >= 0)
  prev_dist = jnp.where(col < WINDOW, dist, 2 * WINDOW)
  lane_q = lax.broadcasted_iota(jnp.int32, (WINDOW, KV_W), 1)

  def operand(ref, j, p):
    g = p // (N_PAIRS // N_KV_HEADS)
    r0 = j * WINDOW
    return jnp.concatenate(
        [ref[2 * g, r0:r0 + 2 * WINDOW, :], ref[2 * g + 1, r0:r0 + 2 * WINDOW, :]], axis=0)

  def scores(j, p):
    r0 = j * WINDOW
    return _dot_t(q_s[r0:r0 + WINDOW, p * KV_W:(p + 1) * KV_W], operand(kz_s, j, p))

  def softmax(s, j, p):
    prev_limit = jnp.where(first_tile, 0, WINDOW) if j == 0 else WINDOW
    valid = own_ok | (prev_dist < prev_limit)
    probs, inv = [], []
    for hh in range(2):
      h = 2 * p + hh
      sink = sinks_ref[h] * LOG2E
      sh = s[:, hh * 2 * WINDOW:(hh + 1) * 2 * WINDOW]
      sh = jnp.where(valid, sh - bias_s[h], NEG)
      m = jnp.maximum(jnp.max(sh, axis=-1, keepdims=True), sink)
      e = jnp.exp2(sh - m)
      denom = jnp.sum(e, axis=-1, keepdims=True) + jnp.exp2(sink - m)
      probs.append(e.astype(BF16))
      inv.append(1.0 / denom)
    return jnp.concatenate(probs, axis=-1), jnp.where(lane_q < HEAD_DIM, inv[0], inv[1])

  def weighted_values(j, p, probs, inv):
    r0 = j * WINDOW
    o = _dot(probs, operand(vz_s, j, p)) * inv
    attn_s[r0:r0 + WINDOW, p * KV_W:(p + 1) * KV_W] = o.astype(BF16)

  lru_state = {}
  n_groups = LRU_ROWS // SUBLANES

  lane_groups = LRU_W // LANES

  def put_rows(ref, first_row, val):
    for g in range(lane_groups):
      ref[g, first_row:first_row + val.shape[0], :] = val[:, g * LANES:(g + 1) * LANES]

  def phase_view(ref, first_row):
    return jnp.concatenate(
        [ref[g, pl.ds(first_row, n_groups, stride=SUBLANES), :] for g in range(lane_groups)], axis=-1)

  def phase_store(ref, first_row, val):
    for g in range(lane_groups):
      ref[g, pl.ds(first_row, n_groups, stride=SUBLANES), :] = val[:, g * LANES:(g + 1) * LANES]

  def lru_conv(rc):
    if rc == 0:
      put_rows(xpad_s, SUBLANES, lru_in["xb"])
      lru_state["h"] = hcar_s[0:1, :]
    c0 = SUBLANES + rc * LRU_ROWS
    shifted = {d: phase_view(xpad_s, c0 + d) for d in range(1 - CONV_W, SUBLANES)}
    phases = []
    for r in range(SUBLANES):
      xc = convb_ref[...] + convw_ref[CONV_W - 1:CONV_W, :] * shifted[r]
      for kk in range(CONV_W - 1):
        xc = xc + convw_ref[kk:kk + 1, :] * shifted[r - (CONV_W - 1) + kk]
      phases.append(xc)
    lru_state["xc"] = jnp.concatenate(phases, axis=0)

  def lru_recurrence(rc):
    if rc == 0:
      put_rows(yb_s, 0, lru_in["yb"])
    c0 = rc * LRU_ROWS
    a, b = _lru_coeffs(lru_state["xc"], wa_ref, ba_ref, wi_ref, bi_ref, lam_ref)
    phase = lambda t, r: t[r * n_groups:(r + 1) * n_groups, :]
    a_cum, b_cum = [phase(a, 0)], [phase(b, 0)]
    for r in range(1, SUBLANES):
      b_cum.append(phase(a, r) * b_cum[-1] + phase(b, r))
      a_cum.append(phase(a, r) * a_cum[-1])
    h = lru_state["h"]
    for gi in range(n_groups):
      hin_s[gi:gi + 1, :] = h
      h = a_cum[-1][gi:gi + 1, :] * h + b_cum[-1][gi:gi + 1, :]
    lru_state["h"] = h
    h_in = hin_s[...]
    for r in range(SUBLANES):
      hs = a_cum[r] * h_in + b_cum[r]
      gate = jax.nn.gelu(phase_view(yb_s, c0 + r))
      phase_store(lru_s, c0 + r, hs * gate)

  problems = [(j, p) for j in range(tq // WINDOW) for p in range(N_PAIRS)]
  raw = {i: scores(*problems[i]) for i in range(2)}
  ready = {}
  for i, (j, p) in enumerate(problems):
    if i + 2 < len(problems):
      raw[i + 2] = scores(*problems[i + 2])
    ready[i] = softmax(raw.pop(i), j, p)
    filler(i)
    if i % N_PAIRS == 1:
      lru_recurrence(i // N_PAIRS)
    if i >= 1:
      weighted_values(*problems[i - 1], *ready.pop(i - 1))
    if i % N_PAIRS == 0:
      lru_conv(i // N_PAIRS)
  weighted_values(*problems[-1], *ready.pop(len(problems) - 1))

  kz_s[:, 0:WINDOW, :] = kz_s[:, tq:tq + WINDOW, :]
  vz_s[:, 0:WINDOW, :] = vz_s[:, tq:tq + WINDOW, :]
  xb = lru_in["xb"]
  h = lru_state["h"]
  xpad_s[:, 0:SUBLANES, :] = xpad_s[:, tq:tq + SUBLANES, :]
  hcar_s[...] = jnp.broadcast_to(h, (SUBLANES, LRU_W))

  @pl.when(last_tile)
  def _():
    nk_ref[0] = k[tq - WINDOW:tq, :]
    nv_ref[0] = v[tq - WINDOW:tq, :]
    nconv_ref[0] = xb[tq - (CONV_W - 1):tq, :]
    nh_ref[0] = h

  project_attention(1)
  finish_mixer(1)
  if run_ffn:
    ffn.finish()


def _whole(space):
  return pl.BlockSpec(memory_space=space)


def _prompt_call(x, sinks, weights):
  batch, seq, _ = x.shape
  tq = PROMPT_TILE
  n_t = seq // tq
  n_tiles = batch * n_t
  vm = _whole(pltpu.VMEM)

  def mixer_tile(s):
    return jnp.minimum(s, n_tiles - 1)

  def ffn_tile(s):
    return jnp.maximum(s - 1, 0)

  in_specs = [_whole(pltpu.SMEM),
              pl.BlockSpec((1, tq, D_MODEL), lambda s: (mixer_tile(s) // n_t, mixer_tile(s) % n_t, 0))]
  in_specs += [vm] * len(weights)
  out_shape = (
      jax.ShapeDtypeStruct((batch, seq, D_MODEL), F32),
      jax.ShapeDtypeStruct((batch, WINDOW, KV_W), F32),
      jax.ShapeDtypeStruct((batch, WINDOW, KV_W), F32),
      jax.ShapeDtypeStruct((batch, CONV_W - 1, LRU_W), F32),
      jax.ShapeDtypeStruct((batch, 1, LRU_W), F32),
  )
  state_map = lambda s: (mixer_tile(s) // n_t, 0, 0)
  out_specs = (
      pl.BlockSpec((1, tq, D_MODEL), lambda s: (ffn_tile(s) // n_t, ffn_tile(s) % n_t, 0)),
      pl.BlockSpec((1, WINDOW, KV_W), state_map),
      pl.BlockSpec((1, WINDOW, KV_W), state_map),
      pl.BlockSpec((1, CONV_W - 1, LRU_W), state_map),
      pl.BlockSpec((1, 1, LRU_W), state_map),
  )
  scratch = [
      pltpu.VMEM((tq, ATTN_W), BF16),
      pltpu.VMEM((4, WINDOW + tq, KV_W), BF16),
      pltpu.VMEM((4, WINDOW + tq, KV_W), BF16),
      pltpu.VMEM((LRU_W // LANES, SUBLANES + tq, LANES), F32),
      pltpu.VMEM((LRU_W // LANES, tq, LANES), F32),
      pltpu.VMEM((LRU_ROWS // SUBLANES, LRU_W), F32),
      pltpu.VMEM((SUBLANES, LRU_W), F32),
      pltpu.VMEM((tq, ATTN_W), BF16),
      pltpu.VMEM((LRU_W // LANES, tq, LANES), F32),
      pltpu.VMEM((2, tq, D_MODEL), F32),
      pltpu.VMEM((tq, D_MODEL), BF16),
      pltpu.VMEM((tq, D_FF), BF16),
      pltpu.VMEM((N_HEADS, WINDOW, 2 * WINDOW), F32),
  ]
  return pl.pallas_call(
      functools.partial(_prompt_kernel, tiles_per_seq=n_t),
      grid=(n_tiles + 1,),
      in_specs=in_specs,
      out_specs=out_specs,
      out_shape=out_shape,
      scratch_shapes=scratch,
      compiler_params=pltpu.CompilerParams(
          dimension_semantics=("arbitrary",),
          vmem_limit_bytes=VMEM_LIMIT_BYTES),
      name="prompt_layer",
  )(sinks, x, *weights)


def _sample_kernel(
    sink_ref, x_ref, ck_ref, cv_ref, cpad_ref, h0_ref, gmix_ref, win_ref, bin_ref, convw_ref,
    convb_ref, wa_ref, ba_ref, wi_ref, bi_ref, lam_ref, wout_ref, bout_ref, gffn_ref, wfi_ref,
    wfo_ref, gfin_ref,
    y_ref, nk_ref, nv_ref, xb_ref, hs_ref,
    q_s, kn_s, vn_s, yb_s, attn_s, act_s):
  n_tok = CONV_W
  rows = x_ref.shape[0]
  step = pl.program_id(0)

  @pl.when(step == 0)
  def _():
    u = _rmsnorm(x_ref[...], gmix_ref[...]).astype(BF16)
    q_s[...] = (_dot(u, win_ref[:, 0:Q_END]) + bin_ref[:, 0:Q_END]) * (HEAD_DIM ** -0.5)
    kv = _dot(u, win_ref[:, Q_END:V_END]) + bin_ref[:, Q_END:V_END]
    kn_s[...] = kv[:, 0:KV_W]
    vn_s[...] = kv[:, KV_W:2 * KV_W]
    xb_ref[...] = _dot(u, win_ref[:, V_END:X_END]) + bin_ref[:, V_END:X_END]
    yb_s[...] = _dot(u, win_ref[:, X_END:IN_W]) + bin_ref[:, X_END:IN_W]

  prow = lax.broadcasted_iota(jnp.int32, (PAIR_ROWS, 2 * WINDOW), 0)
  pcol = lax.broadcasted_iota(jnp.int32, (PAIR_ROWS, 2 * WINDOW), 1)
  seq_r = (prow % SUBLANES) // n_tok
  tok_r = prow % n_tok
  dist_c = WINDOW + tok_r - (pcol % WINDOW)
  valid_c = (seq_r == pcol // WINDOW) & (dist_c >= 0) & (dist_c < WINDOW)
  nrow = lax.broadcasted_iota(jnp.int32, (PAIR_ROWS, LANES), 0)
  ncol = lax.broadcasted_iota(jnp.int32, (PAIR_ROWS, LANES), 1)
  dist_n = (nrow % n_tok) - (ncol % n_tok)
  valid_n = (ncol < SUBLANES) & ((nrow % SUBLANES) // n_tok == ncol // n_tok) & (dist_n >= 0)
  slope_c = jnp.zeros((PAIR_ROWS, 2 * WINDOW), F32)
  slope_n = jnp.zeros((PAIR_ROWS, LANES), F32)
  for h in range(N_HEADS):
    slope_c = jnp.where(prow // SUBLANES == h, _alibi_slope(h), slope_c)
    slope_n = jnp.where(nrow // SUBLANES == h, _alibi_slope(h), slope_n)
  bias_c = slope_c * dist_c.astype(F32)
  bias_n = slope_n * dist_n.astype(F32)
  sink = sink_ref[...]
  pad = jnp.zeros((LANES - SUBLANES, KV_W), F32)
  low_half = lax.broadcasted_iota(jnp.int32, (SUBLANES, KV_W), 1) < HEAD_DIM
  first_row = step * (SAMPLE_SEQS * n_tok)
  kept = WINDOW - n_tok

  def scores(c):
    r0 = pl.multiple_of(first_row + c * SUBLANES, SUBLANES)
    blocks = []
    for h in range(N_HEADS):
      grp = q_s[pl.ds(r0, SUBLANES), (h // 2) * KV_W:(h // 2 + 1) * KV_W]
      to_low = h // GROUP == 0
      if (h % 2 == 0) != to_low:
        grp = pltpu.roll(grp, HEAD_DIM, axis=1)
      blocks.append(jnp.where(low_half == to_low, grp, 0.0))
    lhs = jnp.concatenate(blocks, axis=0).astype(BF16)
    kn = kn_s[pl.ds(r0, SUBLANES), :]
    kc = ck_ref[pl.ds(2 * c, 2)].reshape(2 * WINDOW, KV_W).astype(BF16)
    s_c = jnp.where(valid_c, _dot_t(lhs, kc) - bias_c, NEG)
    s_n = jnp.where(
        valid_n, _dot_t(lhs, jnp.concatenate([kn, pad], axis=0).astype(BF16)) - bias_n, NEG)
    return s_c, s_n

  def softmax(s_c, s_n):
    m = jnp.maximum(
        jnp.maximum(jnp.max(s_c, axis=-1, keepdims=True), jnp.max(s_n, axis=-1, keepdims=True)),
        sink)
    e_c = jnp.exp(s_c - m)
    e_n = jnp.exp(s_n - m)
    denom = (jnp.sum(e_c, axis=-1, keepdims=True) + jnp.sum(e_n, axis=-1, keepdims=True)
             + jnp.exp(sink - m))
    return e_c.astype(BF16), e_n.astype(BF16), 1.0 / denom

  def weighted_values(c, e_c, e_n, inv):
    r0 = pl.multiple_of(first_row + c * SUBLANES, SUBLANES)
    vn = vn_s[pl.ds(r0, SUBLANES), :]
    vc = cv_ref[pl.ds(2 * c, 2)].reshape(2 * WINDOW, KV_W).astype(BF16)
    o = (_dot(e_c, vc) + _dot(e_n, jnp.concatenate([vn, pad], axis=0).astype(BF16))) * inv
    for hp in range(N_PAIRS):
      lo = o[2 * hp * SUBLANES:(2 * hp + 1) * SUBLANES, :]
      hi = o[(2 * hp + 1) * SUBLANES:(2 * hp + 2) * SUBLANES, :]
      if (2 * hp) // GROUP == 0:
        hi = pltpu.roll(hi, HEAD_DIM, axis=1)
      else:
        lo = pltpu.roll(lo, HEAD_DIM, axis=1)
      attn_s[pl.ds(r0, SUBLANES), hp * KV_W:(hp + 1) * KV_W] = jnp.where(low_half, lo, hi)

  def shift_caches(c):
    r0 = pl.multiple_of(first_row + c * SUBLANES, SUBLANES)
    for new_ref, cache_ref, out_ref in ((kn_s, ck_ref, nk_ref), (vn_s, cv_ref, nv_ref)):
      out_ref[pl.ds(2 * c, 2), 0:kept, :] = cache_ref[pl.ds(2 * c, 2), n_tok:WINDOW, :]
      out_ref[pl.ds(2 * c, 1), kept:WINDOW, :] = new_ref[pl.ds(r0, n_tok), :].reshape(1, n_tok, KV_W)
      out_ref[pl.ds(2 * c + 1, 1), kept:WINDOW, :] = (
          new_ref[pl.ds(r0 + n_tok, n_tok), :].reshape(1, n_tok, KV_W))

  def pair_group(gi, carry):
    cs = [gi * SAMPLE_PAIRS_PER_ITER + i for i in range(SAMPLE_PAIRS_PER_ITER)]
    raw = [scores(c) for c in cs]
    probs = [softmax(*s) for s in raw]
    for c, pr in zip(cs, probs):
      weighted_values(c, *pr)
      shift_caches(c)
    return carry

  lax.fori_loop(0, SAMPLE_SEQS // 2 // SAMPLE_PAIRS_PER_ITER, pair_group, 0)

  @pl.when(step == pl.num_programs(0) - 1)
  def _():
    x = x_ref[...]
    xb = xb_ref[...]
    tok = lax.broadcasted_iota(jnp.int32, (rows, LRU_W), 0) % n_tok
    cpad = cpad_ref[...]
    xc = convb_ref[...] + convw_ref[CONV_W - 1:CONV_W, :] * xb
    for shift in range(1, CONV_W):
      saved = cpad if shift == CONV_W - 1 else pltpu.roll(cpad, rows - (CONV_W - 1 - shift), axis=0)
      prev = jnp.where(tok >= shift, pltpu.roll(xb, shift, axis=0), saved)
      xc = xc + convw_ref[CONV_W - 1 - shift:CONV_W - shift, :] * prev

    a, b = _lru_coeffs(xc, wa_ref, ba_ref, wi_ref, bi_ref, lam_ref)
    a, b = _scan_within_groups(a, b, n_tok)
    hs = a * h0_ref[...] + b
    hs_ref[...] = hs
    lru = hs * jax.nn.gelu(yb_s[...])

    hres = (x + _dot(attn_s[...].astype(BF16), wout_ref[0:ATTN_W, :])
            + _dot(lru.astype(BF16), wout_ref[ATTN_W:ATTN_W + LRU_W, :]) + bout_ref[...])
    hn = _rmsnorm(hres, gffn_ref[...]).astype(BF16)

    def write_output(first_row, out_rows):
      y_ref[first_row:first_row + out_rows.shape[0], :] = out_rows

    _Ffn(hres, hn, wfi_ref, wfo_ref, gfin_ref, act_s, write_output).finish()


def _sample_call(x, ck, cv, cpad, h0rep, sink_rows, weights):
  n_rows = x.shape[0]
  n_seqs = ck.shape[0]
  vm = _whole(pltpu.VMEM)
  cache_spec = pl.BlockSpec((SAMPLE_SEQS, WINDOW, KV_W), lambda i: (i, 0, 0))
  in_specs = [vm, vm, cache_spec, cache_spec, vm, vm] + [vm] * len(weights)
  out_shape = (
      jax.ShapeDtypeStruct((n_rows, D_MODEL), F32),
      jax.ShapeDtypeStruct((n_seqs, WINDOW, KV_W), F32),
      jax.ShapeDtypeStruct((n_seqs, WINDOW, KV_W), F32),
      jax.ShapeDtypeStruct((n_rows, LRU_W), F32),
      jax.ShapeDtypeStruct((n_rows, LRU_W), F32),
  )
  out_specs = (vm, cache_spec, cache_spec, vm, vm)
  scratch = [
      pltpu.VMEM((n_rows, ATTN_W), F32),
      pltpu.VMEM((n_rows, KV_W), F32),
      pltpu.VMEM((n_rows, KV_W), F32),
      pltpu.VMEM((n_rows, LRU_W), F32),
      pltpu.VMEM((n_rows, ATTN_W), F32),
      pltpu.VMEM((n_rows, D_FF), BF16),
  ]
  return pl.pallas_call(
      _sample_kernel,
      grid=(n_seqs // SAMPLE_SEQS,),
      in_specs=in_specs,
      out_specs=out_specs,
      out_shape=out_shape,
      scratch_shapes=scratch,
      compiler_params=pltpu.CompilerParams(
          dimension_semantics=("arbitrary",),
          vmem_limit_bytes=VMEM_LIMIT_BYTES),
      name="sample_layer",
  )(sink_rows, x, ck, cv, cpad, h0rep, *weights)


def _block_diag_halves(w):
  per = LRU_BLOCKS // 2
  blocks = w.reshape(2, per, 1, LRU_BW, LRU_BW)
  on_diag = jnp.eye(per, dtype=bool).reshape(1, per, per, 1, 1)
  tiles = jnp.where(on_diag, blocks, 0.0)
  return tiles.transpose(0, 1, 3, 2, 4).reshape(2, per * LRU_BW, per * LRU_BW).astype(BF16)


def kernel(x_prompt, x_sample, cache_k, cache_v, state_conv, state_h, g_mix, w_in, b_in,
           attn_sinks, conv_w, conv_b, w_a, b_a, w_i, b_i, lam, w_out, b_out, g_ffn,
           w_ffn_in, w_ffn_out, g_final):
  depth = g_mix.shape[0]
  assert depth == 1
  batch, seq, _ = x_prompt.shape
  dec_batch, dec_seq, _ = x_sample.shape
  assert seq % PROMPT_TILE == 0 and dec_seq == CONV_W and dec_batch % SAMPLE_SEQS == 0

  row = lambda p: p.reshape(1, -1)
  weights = [row(g_mix[0]), w_in[0].astype(BF16), row(b_in[0]), conv_w[0], row(conv_b[0]),
             _block_diag_halves(w_a[0]), row(b_a[0]), _block_diag_halves(w_i[0]), row(b_i[0]),
             row(lam[0]), w_out[0].astype(BF16), row(b_out[0]), row(g_ffn[0]),
             w_ffn_in[0].astype(BF16), w_ffn_out[0].astype(BF16), row(g_final)]

  y_p, nk_p, nv_p, nconv_p, nh_p = _prompt_call(x_prompt, attn_sinks[0], weights)

  n_rows = dec_batch * dec_seq
  cpad = jnp.pad(state_conv[0], ((0, 0), (0, 1), (0, 0))).reshape(n_rows, LRU_W)
  h0rep = jnp.repeat(state_h[0], dec_seq, axis=0)
  sink_rows = jnp.repeat(attn_sinks[0], SUBLANES).reshape(PAIR_ROWS, 1)
  y_s, nk_s, nv_s, xb_s, hs_s = _sample_call(
      x_sample.reshape(n_rows, D_MODEL), cache_k[0].reshape(dec_batch, WINDOW, KV_W),
      cache_v[0].reshape(dec_batch, WINDOW, KV_W), cpad, h0rep, sink_rows, weights)
  nconv_s = xb_s.reshape(dec_batch, dec_seq, LRU_W)[:, dec_seq - (CONV_W - 1):]
  nh_s = hs_s.reshape(dec_batch, dec_seq, LRU_W)[:, dec_seq - 1]

  pkv = (1, batch, WINDOW, N_KV_HEADS, HEAD_DIM)
  skv = (1, dec_batch, WINDOW, N_KV_HEADS, HEAD_DIM)
  return (y_p, y_s.reshape(dec_batch, dec_seq, D_MODEL),
          nk_p.reshape(pkv), nv_p.reshape(pkv), nconv_p[None], nh_p.reshape(1, batch, LRU_W),
          nk_s.reshape(skv), nv_s.reshape(skv), nconv_s[None], nh_s[None])
```

```python
import functools

import jax
import jax.numpy as jnp
from jax import lax
from jax.experimental import pallas as pl
from jax.experimental.pallas import tpu as pltpu

D_MODEL = 1024
N_HEADS = 8
N_KV_HEADS = 2
HEAD_DIM = 64
GROUP = N_HEADS // N_KV_HEADS
ATTN_W = N_HEADS * HEAD_DIM
KV_W = N_KV_HEADS * HEAD_DIM
WINDOW = 128
LRU_W = D_MODEL // 2
LRU_BLOCKS = 8
LRU_BW = LRU_W // LRU_BLOCKS
CONV_W = 4
LRU_C = 8.0
IN_W = ATTN_W + 2 * KV_W + 2 * LRU_W
Q_END = ATTN_W
K_END = Q_END + KV_W
V_END = K_END + KV_W
X_END = V_END + LRU_W
D_FF = 2816
EPS = 1e-6
NEG = -1e30
LOG2E = 1.4426950408889634

SUBLANES = 8
LANES = 128
VMEM_LIMIT_BYTES = 56 * 1024 * 1024

PROMPT_TILE = 512
FFN_CHUNK = 256
DOWN_ROWS = 256
LRU_ROWS = WINDOW
N_PAIRS = N_HEADS // 2
SAMPLE_SEQS = 16
PAIR_ROWS = N_HEADS * SUBLANES
SAMPLE_PAIRS_PER_ITER = 8

F32 = jnp.float32
BF16 = jnp.bfloat16


def _dot(a, b):
  return jnp.dot(a, b, preferred_element_type=F32)


def _dot_t(a, b):
  return lax.dot_general(a, b, (((1,), (1,)), ((), ())), preferred_element_type=F32)


def _rmsnorm(x, g):
  return x * lax.rsqrt(jnp.mean(x * x, axis=-1, keepdims=True) + EPS) * g


def _alibi_slope(h):
  return 2.0 ** (-8.0 * (h + 1) / N_HEADS)


def _lru_coeffs(xc, wa_ref, ba_ref, wi_ref, bi_ref, lam_ref):
  xcb = xc.astype(BF16)
  half = LRU_W // 2
  lo, hi = xcb[:, :half], xcb[:, half:]
  r = jnp.concatenate([_dot(lo, wa_ref[0]), _dot(hi, wa_ref[1])], axis=-1) + ba_ref[...]
  ig = jnp.concatenate([_dot(lo, wi_ref[0]), _dot(hi, wi_ref[1])], axis=-1) + bi_ref[...]
  r = jax.nn.sigmoid(r)
  ig = jax.nn.sigmoid(ig)
  nlam = -lam_ref[...]
  softplus = jnp.maximum(nlam, 0.0) + jnp.log1p(jnp.exp(-jnp.abs(nlam)))
  log_a = r * (-LRU_C * softplus)
  a = jnp.exp(log_a)
  gain_sq = -jnp.tanh(log_a) * (a * a + 1.0)
  gain = jnp.where(gain_sq > 0.0, gain_sq * lax.rsqrt(gain_sq), 0.0)
  return a, gain * (ig * xc)


def _scan_within_groups(a, b, period):
  t, w = a.shape
  a3 = a.reshape(t // SUBLANES, SUBLANES, w)
  b3 = b.reshape(t // SUBLANES, SUBLANES, w)
  pos = lax.broadcasted_iota(jnp.int32, (1, SUBLANES, w), 1) % period
  d = 1
  while d < period:
    keep = pos >= d
    a_prev = jnp.where(keep, pltpu.roll(a3, d, axis=1), 1.0)
    b_prev = jnp.where(keep, pltpu.roll(b3, d, axis=1), 0.0)
    b3 = a3 * b_prev + b3
    a3 = a3 * a_prev
    d *= 2
  return a3.reshape(t, w), b3.reshape(t, w)


class _Ffn:
  def __init__(self, hres, hn, wfi_ref, wfo_ref, gfin_ref, act_ref, write):
    self.hres, self.hn, self.write = hres, hn, write
    self.wfi_ref, self.wfo_ref, self.gfin_ref, self.act_ref = wfi_ref, wfo_ref, gfin_ref, act_ref
    self.n_up = D_FF // FFN_CHUNK
    self.down_rows = min(DOWN_ROWS, hres.shape[0])
    self.n_units = self.n_up + hres.shape[0] // self.down_rows
    self.done = 0

  def run(self, n_units=1):
    for _ in range(min(n_units, self.n_units - self.done)):
      if self.done < self.n_up:
        lo = self.done * FFN_CHUNK
        w = jnp.concatenate([self.wfi_ref[:, lo:lo + FFN_CHUNK],
                             self.wfi_ref[:, D_FF + lo:D_FF + lo + FFN_CHUNK]], axis=-1)
        gate_up = _dot(self.hn, w)
        gate, up = gate_up[:, :FFN_CHUNK], gate_up[:, FFN_CHUNK:]
        self.act_ref[:, lo:lo + FFN_CHUNK] = (jax.nn.silu(gate) * up).astype(BF16)
      else:
        r0 = (self.done - self.n_up) * self.down_rows
        r1 = r0 + self.down_rows
        out = self.hres[r0:r1, :] + _dot(self.act_ref[r0:r1, :], self.wfo_ref[...])
        self.write(r0, _rmsnorm(out, self.gfin_ref[...]))
      self.done += 1

  def finish(self):
    self.run(self.n_units)


def _prompt_kernel(
    sinks_ref, x_ref, gmix_ref, win_ref, bin_ref, convw_ref, convb_ref, wa_ref, ba_ref,
    wi_ref, bi_ref, lam_ref, wout_ref, bout_ref, gffn_ref, wfi_ref, wfo_ref, gfin_ref,
    y_ref, nk_ref, nv_ref, nconv_ref, nh_ref,
    q_s, kz_s, vz_s, xpad_s, yb_s, hin_s, hcar_s, attn_s, lru_s, hres_s, hn_s, act_s, bias_s, *,
    tiles_per_seq):
  tq = PROMPT_TILE
  step = pl.program_id(0)
  slot = step % 2
  t_idx = step % tiles_per_seq
  first_tile = t_idx == 0
  last_tile = t_idx == tiles_per_seq - 1

  row = lax.broadcasted_iota(jnp.int32, (WINDOW, 2 * WINDOW), 0)
  col = lax.broadcasted_iota(jnp.int32, (WINDOW, 2 * WINDOW), 1)
  dist = WINDOW + row - col

  @pl.when(step == 0)
  def _():
    hres_s[...] = jnp.zeros((2, tq, D_MODEL), F32)
    hn_s[...] = jnp.zeros((tq, D_MODEL), BF16)
    for h in range(N_HEADS):
      bias_s[h] = (_alibi_slope(h) * LOG2E) * dist.astype(F32)

  @pl.when(first_tile)
  def _():
    kz_s[:, 0:WINDOW, :] = jnp.zeros((4, WINDOW, KV_W), BF16)
    vz_s[:, 0:WINDOW, :] = jnp.zeros((4, WINDOW, KV_W), BF16)
    xpad_s[:, 0:SUBLANES, :] = jnp.zeros((LRU_W // LANES, SUBLANES, LANES), F32)
    hcar_s[...] = jnp.zeros((SUBLANES, LRU_W), F32)

  def write_output(first_row, rows):
    y_ref[0, first_row:first_row + rows.shape[0], :] = rows

  ffn = _Ffn(hres_s[1 - slot], hn_s[...], wfi_ref, wfo_ref, gfin_ref, act_s, write_output)
  ffn.run(1)

  x = x_ref[0]
  u = _rmsnorm(x, gmix_ref[...]).astype(BF16)

  q = _dot(u, win_ref[:, 0:Q_END]) + bin_ref[:, 0:Q_END]
  q_s[...] = (q * (HEAD_DIM ** -0.5 * LOG2E)).astype(BF16)
  w_kv = win_ref[:, Q_END:V_END]
  kv = jnp.concatenate([_dot(u[:tq // 2], w_kv), _dot(u[tq // 2:], w_kv)], axis=0)
  kv = kv + bin_ref[:, Q_END:V_END]
  k = kv[:, 0:KV_W]
  v = kv[:, KV_W:2 * KV_W]
  lru_in = {}

  def project_lru_section(name, lo):
    lru_in[name] = _dot(u, win_ref[:, lo:lo + LRU_W]) + bin_ref[:, lo:lo + LRU_W]

  project_lru_section("xb", V_END)

  half = tq // 2
  out_partial = {}

  def project_attention(rows):
    out_partial[rows] = _dot(attn_s[rows * half:(rows + 1) * half, :], wout_ref[0:ATTN_W, :])

  def finish_mixer(rows):
    r0 = rows * half
    lru = jnp.concatenate([lru_s[g, r0:r0 + half, :] for g in range(LRU_W // LANES)], axis=-1)
    hres = (x[r0:r0 + half, :] + out_partial[rows]
            + _dot(lru.astype(BF16), wout_ref[ATTN_W:ATTN_W + LRU_W, :]) + bout_ref[...])
    hres_s[slot, r0:r0 + half, :] = hres
    hn_s[r0:r0 + half, :] = _rmsnorm(hres, gffn_ref[...]).astype(BF16)

  n_problems = (tq // WINDOW) * N_PAIRS
  fixed_fillers = {0: functools.partial(project_lru_section, "yb", X_END),
                   n_problems - 3: functools.partial(project_attention, 0),
                   n_problems - 2: functools.partial(finish_mixer, 0)}

  def filler(i):
    if i in fixed_fillers:
      fixed_fillers[i]()
    else:
      ffn.run(1)

  lane = lax.broadcasted_iota(jnp.int32, (tq, KV_W), 1)
  is_lo = lane < HEAD_DIM
  for src, dst in ((k, kz_s), (v, vz_s)):
    swapped = pltpu.roll(src, HEAD_DIM, axis=1)
    dst[0, WINDOW:WINDOW + tq, :] = jnp.where(is_lo, src, 0.0).astype(BF16)
    dst[1, WINDOW:WINDOW + tq, :] = jnp.where(is_lo, 0.0, swapped).astype(BF16)
    dst[2, WINDOW:WINDOW + tq, :] = jnp.where(is_lo, swapped, 0.0).astype(BF16)
    dst[3, WINDOW:WINDOW + tq, :] = jnp.where(is_lo, 0.0, src).astype(BF16)

  own_ok = (col >= WINDOW) & (dist >= 0)
  prev_dist = jnp.where(col < WINDOW, dist, 2 * WINDOW)
  lane_q = lax.broadcasted_iota(jnp.int32, (WINDOW, KV_W), 1)

  def operand(ref, j, p):
    g = p // (N_PAIRS // N_KV_HEADS)
    r0 = j * WINDOW
    return jnp.concatenate(
        [ref[2 * g, r0:r0 + 2 * WINDOW, :], ref[2 * g + 1, r0:r0 + 2 * WINDOW, :]], axis=0)

  def scores(j, p):
    r0 = j * WINDOW
    return _dot_t(q_s[r0:r0 + WINDOW, p * KV_W:(p + 1) * KV_W], operand(kz_s, j, p))

  def softmax(s, j, p):
    prev_limit = jnp.where(first_tile, 0, WINDOW) if j == 0 else WINDOW
    valid = own_ok | (prev_dist < prev_limit)
    probs, inv = [], []
    for hh in range(2):
      h = 2 * p + hh
      sink = sinks_ref[h] * LOG2E
      sh = s[:, hh * 2 * WINDOW:(hh + 1) * 2 * WINDOW]
      sh = jnp.where(valid, sh - bias_s[h], NEG)
      m = jnp.maximum(jnp.max(sh, axis=-1, keepdims=True), sink)
      e = jnp.exp2(sh - m)
      denom = jnp.sum(e, axis=-1, keepdims=True) + jnp.exp2(sink - m)
      probs.append(e.astype(BF16))
      inv.append(1.0 / denom)
    return jnp.concatenate(probs, axis=-1), jnp.where(lane_q < HEAD_DIM, inv[0], inv[1])

  def weighted_values(j, p, probs, inv):
    r0 = j * WINDOW
    o = _dot(probs, operand(vz_s, j, p)) * inv
    attn_s[r0:r0 + WINDOW, p * KV_W:(p + 1) * KV_W] = o.astype(BF16)

  lru_state = {}
  n_groups = LRU_ROWS // SUBLANES

  lane_groups = LRU_W // LANES

  def put_rows(ref, first_row, val):
    for g in range(lane_groups):
      ref[g, first_row:first_row + val.shape[0], :] = val[:, g * LANES:(g + 1) * LANES]

  def phase_view(ref, first_row):
    return jnp.concatenate(
        [ref[g, pl.ds(first_row, n_groups, stride=SUBLANES), :] for g in range(lane_groups)], axis=-1)

  def phase_store(ref, first_row, val):
    for g in range(lane_groups):
      ref[g, pl.ds(first_row, n_groups, stride=SUBLANES), :] = val[:, g * LANES:(g + 1) * LANES]

  def lru_conv(rc):
    if rc == 0:
      put_rows(xpad_s, SUBLANES, lru_in["xb"])
      lru_state["h"] = hcar_s[0:1, :]
    c0 = SUBLANES + rc * LRU_ROWS
    shifted = {d: phase_view(xpad_s, c0 + d) for d in range(1 - CONV_W, SUBLANES)}
    phases = []
    for r in range(SUBLANES):
      xc = convb_ref[...] + convw_ref[CONV_W - 1:CONV_W, :] * shifted[r]
      for kk in range(CONV_W - 1):
        xc = xc + convw_ref[kk:kk + 1, :] * shifted[r - (CONV_W - 1) + kk]
      phases.append(xc)
    lru_state["xc"] = jnp.concatenate(phases, axis=0)

  def lru_recurrence(rc):
    if rc == 0:
      put_rows(yb_s, 0, lru_in["yb"])
    c0 = rc * LRU_ROWS
    a, b = _lru_coeffs(lru_state["xc"], wa_ref, ba_ref, wi_ref, bi_ref, lam_ref)
    phase = lambda t, r: t[r * n_groups:(r + 1) * n_groups, :]
    a_cum, b_cum = [phase(a, 0)], [phase(b, 0)]
    for r in range(1, SUBLANES):
      b_cum.append(phase(a, r) * b_cum[-1] + phase(b, r))
      a_cum.append(phase(a, r) * a_cum[-1])
    h = lru_state["h"]
    for gi in range(n_groups):
      hin_s[gi:gi + 1, :] = h
      h = a_cum[-1][gi:gi + 1, :] * h + b_cum[-1][gi:gi + 1, :]
    lru_state["h"] = h
    h_in = hin_s[...]
    for r in range(SUBLANES):
      hs = a_cum[r] * h_in + b_cum[r]
      gate = jax.nn.gelu(phase_view(yb_s, c0 + r))
      phase_store(lru_s, c0 + r, hs * gate)

  problems = [(j, p) for j in range(tq // WINDOW) for p in range(N_PAIRS)]
  raw = {i: scores(*problems[i]) for i in range(2)}
  ready = {}
  for i, (j, p) in enumerate(problems):
    if i + 2 < len(problems):
      raw[i + 2] = scores(*problems[i + 2])
    ready[i] = softmax(raw.pop(i), j, p)
    filler(i)
    if i % N_PAIRS == 1:
      lru_recurrence(i // N_PAIRS)
    if i >= 1:
      weighted_values(*problems[i - 1], *ready.pop(i - 1))
    if i % N_PAIRS == 0:
      lru_conv(i // N_PAIRS)
  weighted_values(*problems[-1], *ready.pop(len(problems) - 1))

  kz_s[:, 0:WINDOW, :] = kz_s[:, tq:tq + WINDOW, :]
  vz_s[:, 0:WINDOW, :] = vz_s[:, tq:tq + WINDOW, :]
  xb = lru_in["xb"]
  h = lru_state["h"]
  xpad_s[:, 0:SUBLANES, :] = xpad_s[:, tq:tq + SUBLANES, :]
  hcar_s[...] = jnp.broadcast_to(h, (SUBLANES, LRU_W))

  @pl.when(last_tile)
  def _():
    nk_ref[0] = k[tq - WINDOW:tq, :]
    nv_ref[0] = v[tq - WINDOW:tq, :]
    nconv_ref[0] = xb[tq - (CONV_W - 1):tq, :]
    nh_ref[0] = h

  project_attention(1)
  finish_mixer(1)
  ffn.finish()


def _whole(space):
  return pl.BlockSpec(memory_space=space)


def _prompt_call(x, sinks, weights):
  batch, seq, _ = x.shape
  tq = PROMPT_TILE
  n_t = seq // tq
  n_tiles = batch * n_t
  vm = _whole(pltpu.VMEM)

  def mixer_tile(s):
    return jnp.minimum(s, n_tiles - 1)

  def ffn_tile(s):
    return jnp.maximum(s - 1, 0)

  in_specs = [_whole(pltpu.SMEM),
              pl.BlockSpec((1, tq, D_MODEL), lambda s: (mixer_tile(s) // n_t, mixer_tile(s) % n_t, 0))]
  in_specs += [vm] * len(weights)
  out_shape = (
      jax.ShapeDtypeStruct((batch, seq, D_MODEL), F32),
      jax.ShapeDtypeStruct((batch, WINDOW, KV_W), F32),
      jax.ShapeDtypeStruct((batch, WINDOW, KV_W), F32),
      jax.ShapeDtypeStruct((batch, CONV_W - 1, LRU_W), F32),
      jax.ShapeDtypeStruct((batch, 1, LRU_W), F32),
  )
  state_map = lambda s: (mixer_tile(s) // n_t, 0, 0)
  out_specs = (
      pl.BlockSpec((1, tq, D_MODEL), lambda s: (ffn_tile(s) // n_t, ffn_tile(s) % n_t, 0)),
      pl.BlockSpec((1, WINDOW, KV_W), state_map),
      pl.BlockSpec((1, WINDOW, KV_W), state_map),
      pl.BlockSpec((1, CONV_W - 1, LRU_W), state_map),
      pl.BlockSpec((1, 1, LRU_W), state_map),
  )
  scratch = [
      pltpu.VMEM((tq, ATTN_W), BF16),
      pltpu.VMEM((4, WINDOW + tq, KV_W), BF16),
      pltpu.VMEM((4, WINDOW + tq, KV_W), BF16),
      pltpu.VMEM((LRU_W // LANES, SUBLANES + tq, LANES), F32),
      pltpu.VMEM((LRU_W // LANES, tq, LANES), F32),
      pltpu.VMEM((LRU_ROWS // SUBLANES, LRU_W), F32),
      pltpu.VMEM((SUBLANES, LRU_W), F32),
      pltpu.VMEM((tq, ATTN_W), BF16),
      pltpu.VMEM((LRU_W // LANES, tq, LANES), F32),
      pltpu.VMEM((2, tq, D_MODEL), F32),
      pltpu.VMEM((tq, D_MODEL), BF16),
      pltpu.VMEM((tq, D_FF), BF16),
      pltpu.VMEM((N_HEADS, WINDOW, 2 * WINDOW), F32),
  ]
  return pl.pallas_call(
      functools.partial(_prompt_kernel, tiles_per_seq=n_t),
      grid=(n_tiles + 1,),
      in_specs=in_specs,
      out_specs=out_specs,
      out_shape=out_shape,
      scratch_shapes=scratch,
      compiler_params=pltpu.CompilerParams(
          dimension_semantics=("arbitrary",),
          vmem_limit_bytes=VMEM_LIMIT_BYTES),
      name="prompt_layer",
  )(sinks, x, *weights)


def _sample_kernel(
    sink_ref, x_ref, ck_ref, cv_ref, cpad_ref, h0_ref, gmix_ref, win_ref, bin_ref, convw_ref,
    convb_ref, wa_ref, ba_ref, wi_ref, bi_ref, lam_ref, wout_ref, bout_ref, gffn_ref, wfi_ref,
    wfo_ref, gfin_ref,
    y_ref, nk_ref, nv_ref, xb_ref, hs_ref,
    q_s, kn_s, vn_s, yb_s, attn_s, act_s):
  n_tok = CONV_W
  rows = x_ref.shape[0]
  step = pl.program_id(0)

  @pl.when(step == 0)
  def _():
    u = _rmsnorm(x_ref[...], gmix_ref[...]).astype(BF16)
    q_s[...] = (_dot(u, win_ref[:, 0:Q_END]) + bin_ref[:, 0:Q_END]) * (HEAD_DIM ** -0.5)
    kv = _dot(u, win_ref[:, Q_END:V_END]) + bin_ref[:, Q_END:V_END]
    kn_s[...] = kv[:, 0:KV_W]
    vn_s[...] = kv[:, KV_W:2 * KV_W]
    xb_ref[...] = _dot(u, win_ref[:, V_END:X_END]) + bin_ref[:, V_END:X_END]
    yb_s[...] = _dot(u, win_ref[:, X_END:IN_W]) + bin_ref[:, X_END:IN_W]

  prow = lax.broadcasted_iota(jnp.int32, (PAIR_ROWS, 2 * WINDOW), 0)
  pcol = lax.broadcasted_iota(jnp.int32, (PAIR_ROWS, 2 * WINDOW), 1)
  seq_r = (prow % SUBLANES) // n_tok
  tok_r = prow % n_tok
  dist_c = WINDOW + tok_r - (pcol % WINDOW)
  valid_c = (seq_r == pcol // WINDOW) & (dist_c >= 0) & (dist_c < WINDOW)
  nrow = lax.broadcasted_iota(jnp.int32, (PAIR_ROWS, LANES), 0)
  ncol = lax.broadcasted_iota(jnp.int32, (PAIR_ROWS, LANES), 1)
  dist_n = (nrow % n_tok) - (ncol % n_tok)
  valid_n = (ncol < SUBLANES) & ((nrow % SUBLANES) // n_tok == ncol // n_tok) & (dist_n >= 0)
  slope_c = jnp.zeros((PAIR_ROWS, 2 * WINDOW), F32)
  slope_n = jnp.zeros((PAIR_ROWS, LANES), F32)
  for h in range(N_HEADS):
    slope_c = jnp.where(prow // SUBLANES == h, _alibi_slope(h), slope_c)
    slope_n = jnp.where(nrow // SUBLANES == h, _alibi_slope(h), slope_n)
  bias_c = slope_c * dist_c.astype(F32)
  bias_n = slope_n * dist_n.astype(F32)
  sink = sink_ref[...]
  pad = jnp.zeros((LANES - SUBLANES, KV_W), F32)
  low_half = lax.broadcasted_iota(jnp.int32, (SUBLANES, KV_W), 1) < HEAD_DIM
  first_row = step * (SAMPLE_SEQS * n_tok)
  kept = WINDOW - n_tok

  def scores(c):
    r0 = pl.multiple_of(first_row + c * SUBLANES, SUBLANES)
    blocks = []
    for h in range(N_HEADS):
      grp = q_s[pl.ds(r0, SUBLANES), (h // 2) * KV_W:(h // 2 + 1) * KV_W]
      to_low = h // GROUP == 0
      if (h % 2 == 0) != to_low:
        grp = pltpu.roll(grp, HEAD_DIM, axis=1)
      blocks.append(jnp.where(low_half == to_low, grp, 0.0))
    lhs = jnp.concatenate(blocks, axis=0).astype(BF16)
    kn = kn_s[pl.ds(r0, SUBLANES), :]
    kc = ck_ref[pl.ds(2 * c, 2)].reshape(2 * WINDOW, KV_W).astype(BF16)
    s_c = jnp.where(valid_c, _dot_t(lhs, kc) - bias_c, NEG)
    s_n = jnp.where(
        valid_n, _dot_t(lhs, jnp.concatenate([kn, pad], axis=0).astype(BF16)) - bias_n, NEG)
    return s_c, s_n

  def softmax(s_c, s_n):
    m = jnp.maximum(
        jnp.maximum(jnp.max(s_c, axis=-1, keepdims=True), jnp.max(s_n, axis=-1, keepdims=True)),
        sink)
    e_c = jnp.exp(s_c - m)
    e_n = jnp.exp(s_n - m)
    denom = (jnp.sum(e_c, axis=-1, keepdims=True) + jnp.sum(e_n, axis=-1, keepdims=True)
             + jnp.exp(sink - m))
    return e_c.astype(BF16), e_n.astype(BF16), 1.0 / denom

  def weighted_values(c, e_c, e_n, inv):
    r0 = pl.multiple_of(first_row + c * SUBLANES, SUBLANES)
    vn = vn_s[pl.ds(r0, SUBLANES), :]
    vc = cv_ref[pl.ds(2 * c, 2)].reshape(2 * WINDOW, KV_W).astype(BF16)
    o = (_dot(e_c, vc) + _dot(e_n, jnp.concatenate([vn, pad], axis=0).astype(BF16))) * inv
    for hp in range(N_PAIRS):
      lo = o[2 * hp * SUBLANES:(2 * hp + 1) * SUBLANES, :]
      hi = o[(2 * hp + 1) * SUBLANES:(2 * hp + 2) * SUBLANES, :]
      if (2 * hp) // GROUP == 0:
        hi = pltpu.roll(hi, HEAD_DIM, axis=1)
      else:
        lo = pltpu.roll(lo, HEAD_DIM, axis=1)
      attn_s[pl.ds(r0, SUBLANES), hp * KV_W:(hp + 1) * KV_W] = jnp.where(low_half, lo, hi)

  def shift_caches(c):
    r0 = pl.multiple_of(first_row + c * SUBLANES, SUBLANES)
    for new_ref, cache_ref, out_ref in ((kn_s, ck_ref, nk_ref), (vn_s, cv_ref, nv_ref)):
      out_ref[pl.ds(2 * c, 2), 0:kept, :] = cache_ref[pl.ds(2 * c, 2), n_tok:WINDOW, :]
      out_ref[pl.ds(2 * c, 1), kept:WINDOW, :] = new_ref[pl.ds(r0, n_tok), :].reshape(1, n_tok, KV_W)
      out_ref[pl.ds(2 * c + 1, 1), kept:WINDOW, :] = (
          new_ref[pl.ds(r0 + n_tok, n_tok), :].reshape(1, n_tok, KV_W))

  def pair_group(gi, carry):
    cs = [gi * SAMPLE_PAIRS_PER_ITER + i for i in range(SAMPLE_PAIRS_PER_ITER)]
    raw = [scores(c) for c in cs]
    probs = [softmax(*s) for s in raw]
    for c, pr in zip(cs, probs):
      weighted_values(c, *pr)
      shift_caches(c)
    return carry

  lax.fori_loop(0, SAMPLE_SEQS // 2 // SAMPLE_PAIRS_PER_ITER, pair_group, 0)

  @pl.when(step == pl.num_programs(0) - 1)
  def _():
    x = x_ref[...]
    xb = xb_ref[...]
    tok = lax.broadcasted_iota(jnp.int32, (rows, LRU_W), 0) % n_tok
    cpad = cpad_ref[...]
    xc = convb_ref[...] + convw_ref[CONV_W - 1:CONV_W, :] * xb
    for shift in range(1, CONV_W):
      saved = cpad if shift == CONV_W - 1 else pltpu.roll(cpad, rows - (CONV_W - 1 - shift), axis=0)
      prev = jnp.where(tok >= shift, pltpu.roll(xb, shift, axis=0), saved)
      xc = xc + convw_ref[CONV_W - 1 - shift:CONV_W - shift, :] * prev

    a, b = _lru_coeffs(xc, wa_ref, ba_ref, wi_ref, bi_ref, lam_ref)
    a, b = _scan_within_groups(a, b, n_tok)
    hs = a * h0_ref[...] + b
    hs_ref[...] = hs
    lru = hs * jax.nn.gelu(yb_s[...])

    hres = (x + _dot(attn_s[...].astype(BF16), wout_ref[0:ATTN_W, :])
            + _dot(lru.astype(BF16), wout_ref[ATTN_W:ATTN_W + LRU_W, :]) + bout_ref[...])
    hn = _rmsnorm(hres, gffn_ref[...]).astype(BF16)

    def write_output(first_row, out_rows):
      y_ref[first_row:first_row + out_rows.shape[0], :] = out_rows

    _Ffn(hres, hn, wfi_ref, wfo_ref, gfin_ref, act_s, write_output).finish()


def _sample_call(x, ck, cv, cpad, h0rep, sink_rows, weights):
  n_rows = x.shape[0]
  n_seqs = ck.shape[0]
  vm = _whole(pltpu.VMEM)
  cache_spec = pl.BlockSpec((SAMPLE_SEQS, WINDOW, KV_W), lambda i: (i, 0, 0))
  in_specs = [vm, vm, cache_spec, cache_spec, vm, vm] + [vm] * len(weights)
  out_shape = (
      jax.ShapeDtypeStruct((n_rows, D_MODEL), F32),
      jax.ShapeDtypeStruct((n_seqs, WINDOW, KV_W), F32),
      jax.ShapeDtypeStruct((n_seqs, WINDOW, KV_W), F32),
      jax.ShapeDtypeStruct((n_rows, LRU_W), F32),
      jax.ShapeDtypeStruct((n_rows, LRU_W), F32),
  )
  out_specs = (vm, cache_spec, cache_spec, vm, vm)
  scratch = [
      pltpu.VMEM((n_rows, ATTN_W), F32),
      pltpu.VMEM((n_rows, KV_W), F32),
      pltpu.VMEM((n_rows, KV_W), F32),
      pltpu.VMEM((n_rows, LRU_W), F32),
      pltpu.VMEM((n_rows, ATTN_W), F32),
      pltpu.VMEM((n_rows, D_FF), BF16),
  ]
  return pl.pallas_call(
      _sample_kernel,
      grid=(n_seqs // SAMPLE_SEQS,),
      in_specs=in_specs,
      out_specs=out_specs,
      out_shape=out_shape,
      scratch_shapes=scratch,
      compiler_params=pltpu.CompilerParams(
          dimension_semantics=("arbitrary",),
          vmem_limit_bytes=VMEM_LIMIT_BYTES),
      name="sample_layer",
  )(sink_rows, x, ck, cv, cpad, h0rep, *weights)


def _block_diag_halves(w):
  per = LRU_BLOCKS // 2
  blocks = w.reshape(2, per, 1, LRU_BW, LRU_BW)
  on_diag = jnp.eye(per, dtype=bool).reshape(1, per, per, 1, 1)
  tiles = jnp.where(on_diag, blocks, 0.0)
  return tiles.transpose(0, 1, 3, 2, 4).reshape(2, per * LRU_BW, per * LRU_BW).astype(BF16)


def kernel(x_prompt, x_sample, cache_k, cache_v, state_conv, state_h, g_mix, w_in, b_in,
           attn_sinks, conv_w, conv_b, w_a, b_a, w_i, b_i, lam, w_out, b_out, g_ffn,
           w_ffn_in, w_ffn_out, g_final):
  depth = g_mix.shape[0]
  assert depth == 1
  batch, seq, _ = x_prompt.shape
  dec_batch, dec_seq, _ = x_sample.shape
  assert seq % PROMPT_TILE == 0 and dec_seq == CONV_W and dec_batch % SAMPLE_SEQS == 0

  row = lambda p: p.reshape(1, -1)
  weights = [row(g_mix[0]), w_in[0].astype(BF16), row(b_in[0]), conv_w[0], row(conv_b[0]),
             _block_diag_halves(w_a[0]), row(b_a[0]), _block_diag_halves(w_i[0]), row(b_i[0]),
             row(lam[0]), w_out[0].astype(BF16), row(b_out[0]), row(g_ffn[0]),
             w_ffn_in[0].astype(BF16), w_ffn_out[0].astype(BF16), row(g_final)]

  y_p, nk_p, nv_p, nconv_p, nh_p = _prompt_call(x_prompt, attn_sinks[0], weights)

  n_rows = dec_batch * dec_seq
  cpad = jnp.pad(state_conv[0], ((0, 0), (0, 1), (0, 0))).reshape(n_rows, LRU_W)
  h0rep = jnp.repeat(state_h[0], dec_seq, axis=0)
  sink_rows = jnp.repeat(attn_sinks[0], SUBLANES).reshape(PAIR_ROWS, 1)
  y_s, nk_s, nv_s, xb_s, hs_s = _sample_call(
      x_sample.reshape(n_rows, D_MODEL), cache_k[0].reshape(dec_batch, WINDOW, KV_W),
      cache_v[0].reshape(dec_batch, WINDOW, KV_W), cpad, h0rep, sink_rows, weights)
  nconv_s = xb_s.reshape(dec_batch, dec_seq, LRU_W)[:, dec_seq - (CONV_W - 1):]
  nh_s = hs_s.reshape(dec_batch, dec_seq, LRU_W)[:, dec_seq - 1]

  pkv = (1, batch, WINDOW, N_KV_HEADS, HEAD_DIM)
  skv = (1, dec_batch, WINDOW, N_KV_HEADS, HEAD_DIM)
  return (y_p, y_s.reshape(dec_batch, dec_seq, D_MODEL),
          nk_p.reshape(pkv), nv_p.reshape(pkv), nconv_p[None], nh_p.reshape(1, batch, LRU_W),
          nk_s.reshape(skv), nv_s.reshape(skv), nconv_s[None], nh_s[None])
```

```python
import functools

import jax
import jax.numpy as jnp
from jax import lax
from jax.experimental import pallas as pl
from jax.experimental.pallas import tpu as pltpu

D_MODEL = 1024
N_HEADS = 8
N_KV_HEADS = 2
HEAD_DIM = 64
GROUP = N_HEADS // N_KV_HEADS
ATTN_W = N_HEADS * HEAD_DIM
KV_W = N_KV_HEADS * HEAD_DIM
WINDOW = 128
LRU_W = D_MODEL // 2
LRU_BLOCKS = 8
LRU_BW = LRU_W // LRU_BLOCKS
CONV_W = 4
LRU_C = 8.0
IN_W = ATTN_W + 2 * KV_W + 2 * LRU_W
Q_END = ATTN_W
K_END = Q_END + KV_W
V_END = K_END + KV_W
X_END = V_END + LRU_W
D_FF = 2816
EPS = 1e-6
NEG = -1e30
LOG2E = 1.4426950408889634

SUBLANES = 8
LANES = 128
VMEM_LIMIT_BYTES = 56 * 1024 * 1024

PROMPT_TILE = 512
FFN_CHUNK = 256
DOWN_ROWS = 256
LRU_ROWS = 64
N_PAIRS = N_HEADS // 2
SAMPLE_SEQS = 16
PAIR_ROWS = N_HEADS * SUBLANES
SAMPLE_PAIRS_PER_ITER = 8

F32 = jnp.float32
BF16 = jnp.bfloat16


def _dot(a, b):
  return jnp.dot(a, b, preferred_element_type=F32)


def _dot_t(a, b):
  return lax.dot_general(a, b, (((1,), (1,)), ((), ())), preferred_element_type=F32)


def _rmsnorm(x, g):
  return x * lax.rsqrt(jnp.mean(x * x, axis=-1, keepdims=True) + EPS) * g


def _alibi_slope(h):
  return 2.0 ** (-8.0 * (h + 1) / N_HEADS)


def _lru_coeffs(xc, wa_ref, ba_ref, wi_ref, bi_ref, lam_ref):
  xcb = xc.astype(BF16)
  half = LRU_W // 2
  lo, hi = xcb[:, :half], xcb[:, half:]
  r = jnp.concatenate([_dot(lo, wa_ref[0]), _dot(hi, wa_ref[1])], axis=-1) + ba_ref[...]
  ig = jnp.concatenate([_dot(lo, wi_ref[0]), _dot(hi, wi_ref[1])], axis=-1) + bi_ref[...]
  r = jax.nn.sigmoid(r)
  ig = jax.nn.sigmoid(ig)
  nlam = -lam_ref[...]
  softplus = jnp.maximum(nlam, 0.0) + jnp.log1p(jnp.exp(-jnp.abs(nlam)))
  log_a = r * (-LRU_C * softplus)
  a = jnp.exp(log_a)
  gain_sq = -jnp.tanh(log_a) * (a * a + 1.0)
  gain = jnp.where(gain_sq > 0.0, gain_sq * lax.rsqrt(gain_sq), 0.0)
  return a, gain * (ig * xc)


def _scan_within_groups(a, b, period):
  t, w = a.shape
  a3 = a.reshape(t // SUBLANES, SUBLANES, w)
  b3 = b.reshape(t // SUBLANES, SUBLANES, w)
  pos = lax.broadcasted_iota(jnp.int32, (1, SUBLANES, w), 1) % period
  d = 1
  while d < period:
    keep = pos >= d
    a_prev = jnp.where(keep, pltpu.roll(a3, d, axis=1), 1.0)
    b_prev = jnp.where(keep, pltpu.roll(b3, d, axis=1), 0.0)
    b3 = a3 * b_prev + b3
    a3 = a3 * a_prev
    d *= 2
  return a3.reshape(t, w), b3.reshape(t, w)


class _Ffn:
  def __init__(self, hres, hn, wfi_ref, wfo_ref, gfin_ref, act_ref, write):
    self.hres, self.hn, self.write = hres, hn, write
    self.wfi_ref, self.wfo_ref, self.gfin_ref, self.act_ref = wfi_ref, wfo_ref, gfin_ref, act_ref
    self.n_up = D_FF // FFN_CHUNK
    self.down_rows = min(DOWN_ROWS, hres.shape[0])
    self.n_units = self.n_up + hres.shape[0] // self.down_rows
    self.done = 0

  def run(self, n_units=1):
    for _ in range(min(n_units, self.n_units - self.done)):
      if self.done < self.n_up:
        lo = self.done * FFN_CHUNK
        w = jnp.concatenate([self.wfi_ref[:, lo:lo + FFN_CHUNK],
                             self.wfi_ref[:, D_FF + lo:D_FF + lo + FFN_CHUNK]], axis=-1)
        gate_up = _dot(self.hn, w)
        gate, up = gate_up[:, :FFN_CHUNK], gate_up[:, FFN_CHUNK:]
        self.act_ref[:, lo:lo + FFN_CHUNK] = (jax.nn.silu(gate) * up).astype(BF16)
      else:
        r0 = (self.done - self.n_up) * self.down_rows
        r1 = r0 + self.down_rows
        out = self.hres[r0:r1, :] + _dot(self.act_ref[r0:r1, :], self.wfo_ref[...])
        self.write(r0, _rmsnorm(out, self.gfin_ref[...]))
      self.done += 1

  def finish(self):
    self.run(self.n_units)


def _prompt_kernel(
    sinks_ref, x_ref, gmix_ref, win_ref, bin_ref, convw_ref, convb_ref, wa_ref, ba_ref,
    wi_ref, bi_ref, lam_ref, wout_ref, bout_ref, gffn_ref, wfi_ref, wfo_ref, gfin_ref,
    y_ref, nk_ref, nv_ref, nconv_ref, nh_ref,
    q_s, kz_s, vz_s, xpad_s, yb_s, hin_s, hcar_s, attn_s, lru_s, hres_s, hn_s, act_s, bias_s, *,
    tiles_per_seq):
  tq = PROMPT_TILE
  step = pl.program_id(0)
  slot = step % 2
  t_idx = step % tiles_per_seq
  first_tile = t_idx == 0
  last_tile = t_idx == tiles_per_seq - 1

  row = lax.broadcasted_iota(jnp.int32, (WINDOW, 2 * WINDOW), 0)
  col = lax.broadcasted_iota(jnp.int32, (WINDOW, 2 * WINDOW), 1)
  dist = WINDOW + row - col

  @pl.when(step == 0)
  def _():
    hres_s[...] = jnp.zeros((2, tq, D_MODEL), F32)
    hn_s[...] = jnp.zeros((tq, D_MODEL), BF16)
    for h in range(N_HEADS):
      bias_s[h] = (_alibi_slope(h) * LOG2E) * dist.astype(F32)

  @pl.when(first_tile)
  def _():
    kz_s[:, 0:WINDOW, :] = jnp.zeros((4, WINDOW, KV_W), BF16)
    vz_s[:, 0:WINDOW, :] = jnp.zeros((4, WINDOW, KV_W), BF16)
    xpad_s[:, 0:SUBLANES, :] = jnp.zeros((LRU_W // LANES, SUBLANES, LANES), F32)
    hcar_s[...] = jnp.zeros((SUBLANES, LRU_W), F32)

  def write_output(first_row, rows):
    y_ref[0, first_row:first_row + rows.shape[0], :] = rows

  ffn = _Ffn(hres_s[1 - slot], hn_s[...], wfi_ref, wfo_ref, gfin_ref, act_s, write_output)
  ffn.run(1)

  x = x_ref[0]
  u = _rmsnorm(x, gmix_ref[...]).astype(BF16)

  q = _dot(u, win_ref[:, 0:Q_END]) + bin_ref[:, 0:Q_END]
  q_s[...] = (q * (HEAD_DIM ** -0.5 * LOG2E)).astype(BF16)
  w_kv = win_ref[:, Q_END:V_END]
  kv = jnp.concatenate([_dot(u[:tq // 2], w_kv), _dot(u[tq // 2:], w_kv)], axis=0)
  kv = kv + bin_ref[:, Q_END:V_END]
  k = kv[:, 0:KV_W]
  v = kv[:, KV_W:2 * KV_W]
  lru_in = {}

  def project_lru_section(name, lo):
    lru_in[name] = _dot(u, win_ref[:, lo:lo + LRU_W]) + bin_ref[:, lo:lo + LRU_W]

  half = tq // 2
  out_partial = {}

  def project_attention(rows):
    out_partial[rows] = _dot(attn_s[rows * half:(rows + 1) * half, :], wout_ref[0:ATTN_W, :])

  def finish_mixer(rows):
    r0 = rows * half
    lru = jnp.concatenate([lru_s[g, r0:r0 + half, :] for g in range(LRU_W // LANES)], axis=-1)
    hres = (x[r0:r0 + half, :] + out_partial[rows]
            + _dot(lru.astype(BF16), wout_ref[ATTN_W:ATTN_W + LRU_W, :]) + bout_ref[...])
    hres_s[slot, r0:r0 + half, :] = hres
    hn_s[r0:r0 + half, :] = _rmsnorm(hres, gffn_ref[...]).astype(BF16)

  n_problems = (tq // WINDOW) * N_PAIRS
  fixed_fillers = {0: functools.partial(project_lru_section, "xb", V_END),
                   1: functools.partial(project_lru_section, "yb", X_END),
                   n_problems - 3: functools.partial(project_attention, 0),
                   n_problems - 2: functools.partial(finish_mixer, 0)}

  def filler(i):
    if i in fixed_fillers:
      fixed_fillers[i]()
    else:
      ffn.run(1)

  lane = lax.broadcasted_iota(jnp.int32, (tq, KV_W), 1)
  is_lo = lane < HEAD_DIM
  for src, dst in ((k, kz_s), (v, vz_s)):
    swapped = pltpu.roll(src, HEAD_DIM, axis=1)
    dst[0, WINDOW:WINDOW + tq, :] = jnp.where(is_lo, src, 0.0).astype(BF16)
    dst[1, WINDOW:WINDOW + tq, :] = jnp.where(is_lo, 0.0, swapped).astype(BF16)
    dst[2, WINDOW:WINDOW + tq, :] = jnp.where(is_lo, swapped, 0.0).astype(BF16)
    dst[3, WINDOW:WINDOW + tq, :] = jnp.where(is_lo, 0.0, src).astype(BF16)

  own_ok = (col >= WINDOW) & (dist >= 0)
  prev_dist = jnp.where(col < WINDOW, dist, 2 * WINDOW)
  lane_q = lax.broadcasted_iota(jnp.int32, (WINDOW, KV_W), 1)

  def operand(ref, j, p):
    g = p // (N_PAIRS // N_KV_HEADS)
    r0 = j * WINDOW
    return jnp.concatenate(
        [ref[2 * g, r0:r0 + 2 * WINDOW, :], ref[2 * g + 1, r0:r0 + 2 * WINDOW, :]], axis=0)

  def scores(j, p):
    r0 = j * WINDOW
    return _dot_t(q_s[r0:r0 + WINDOW, p * KV_W:(p + 1) * KV_W], operand(kz_s, j, p))

  def softmax(s, j, p):
    prev_limit = jnp.where(first_tile, 0, WINDOW) if j == 0 else WINDOW
    valid = own_ok | (prev_dist < prev_limit)
    probs, inv = [], []
    for hh in range(2):
      h = 2 * p + hh
      sink = sinks_ref[h] * LOG2E
      sh = s[:, hh * 2 * WINDOW:(hh + 1) * 2 * WINDOW]
      sh = jnp.where(valid, sh - bias_s[h], NEG)
      m = jnp.maximum(jnp.max(sh, axis=-1, keepdims=True), sink)
      e = jnp.exp2(sh - m)
      denom = jnp.sum(e, axis=-1, keepdims=True) + jnp.exp2(sink - m)
      probs.append(e.astype(BF16))
      inv.append(1.0 / denom)
    return jnp.concatenate(probs, axis=-1), jnp.where(lane_q < HEAD_DIM, inv[0], inv[1])

  def weighted_values(j, p, probs, inv):
    r0 = j * WINDOW
    o = _dot(probs, operand(vz_s, j, p)) * inv
    attn_s[r0:r0 + WINDOW, p * KV_W:(p + 1) * KV_W] = o.astype(BF16)

  lru_state = {}
  n_groups = LRU_ROWS // SUBLANES

  lane_groups = LRU_W // LANES

  def put_rows(ref, first_row, val):
    for g in range(lane_groups):
      ref[g, first_row:first_row + val.shape[0], :] = val[:, g * LANES:(g + 1) * LANES]

  def phase_view(ref, first_row):
    return jnp.concatenate(
        [ref[g, pl.ds(first_row, n_groups, stride=SUBLANES), :] for g in range(lane_groups)], axis=-1)

  def phase_store(ref, first_row, val):
    for g in range(lane_groups):
      ref[g, pl.ds(first_row, n_groups, stride=SUBLANES), :] = val[:, g * LANES:(g + 1) * LANES]

  def lru_conv(rc):
    if rc == 0:
      put_rows(xpad_s, SUBLANES, lru_in["xb"])
      lru_state["h"] = hcar_s[0:1, :]
    c0 = SUBLANES + rc * LRU_ROWS
    shifted = {d: phase_view(xpad_s, c0 + d) for d in range(1 - CONV_W, SUBLANES)}
    phases = []
    for r in range(SUBLANES):
      xc = convb_ref[...] + convw_ref[CONV_W - 1:CONV_W, :] * shifted[r]
      for kk in range(CONV_W - 1):
        xc = xc + convw_ref[kk:kk + 1, :] * shifted[r - (CONV_W - 1) + kk]
      phases.append(xc)
    lru_state["xc"] = jnp.concatenate(phases, axis=0)

  def lru_recurrence(rc):
    if rc == 0:
      put_rows(yb_s, 0, lru_in["yb"])
    c0 = rc * LRU_ROWS
    a, b = _lru_coeffs(lru_state["xc"], wa_ref, ba_ref, wi_ref, bi_ref, lam_ref)
    phase = lambda t, r: t[r * n_groups:(r + 1) * n_groups, :]
    a_cum, b_cum = [phase(a, 0)], [phase(b, 0)]
    for r in range(1, SUBLANES):
      b_cum.append(phase(a, r) * b_cum[-1] + phase(b, r))
      a_cum.append(phase(a, r) * a_cum[-1])
    h = lru_state["h"]
    for gi in range(n_groups):
      hin_s[gi:gi + 1, :] = h
      h = a_cum[-1][gi:gi + 1, :] * h + b_cum[-1][gi:gi + 1, :]
    lru_state["h"] = h
    h_in = hin_s[...]
    for r in range(SUBLANES):
      hs = a_cum[r] * h_in + b_cum[r]
      gate = jax.nn.gelu(phase_view(yb_s, c0 + r))
      phase_store(lru_s, c0 + r, hs * gate)

  problems = [(j, p) for j in range(tq // WINDOW) for p in range(N_PAIRS)]
  lru_period = len(problems) // (tq // LRU_ROWS)
  raw = {i: scores(*problems[i]) for i in range(2)}
  ready = {}
  for i, (j, p) in enumerate(problems):
    if i + 2 < len(problems):
      raw[i + 2] = scores(*problems[i + 2])
    ready[i] = softmax(raw.pop(i), j, p)
    filler(i)
    if i % lru_period == 1:
      lru_recurrence(i // lru_period)
    if i >= 1:
      weighted_values(*problems[i - 1], *ready.pop(i - 1))
    if i % lru_period == 0:
      lru_conv(i // lru_period)
  weighted_values(*problems[-1], *ready.pop(len(problems) - 1))

  kz_s[:, 0:WINDOW, :] = kz_s[:, tq:tq + WINDOW, :]
  vz_s[:, 0:WINDOW, :] = vz_s[:, tq:tq + WINDOW, :]
  xb = lru_in["xb"]
  h = lru_state["h"]
  xpad_s[:, 0:SUBLANES, :] = xpad_s[:, tq:tq + SUBLANES, :]
  hcar_s[...] = jnp.broadcast_to(h, (SUBLANES, LRU_W))

  @pl.when(last_tile)
  def _():
    nk_ref[0] = k[tq - WINDOW:tq, :]
    nv_ref[0] = v[tq - WINDOW:tq, :]
    nconv_ref[0] = xb[tq - (CONV_W - 1):tq, :]
    nh_ref[0] = h

  project_attention(1)
  finish_mixer(1)
  ffn.finish()


def _whole(space):
  return pl.BlockSpec(memory_space=space)


def _prompt_call(x, sinks, weights):
  batch, seq, _ = x.shape
  tq = PROMPT_TILE
  n_t = seq // tq
  n_tiles = batch * n_t
  vm = _whole(pltpu.VMEM)

  def mixer_tile(s):
    return jnp.minimum(s, n_tiles - 1)

  def ffn_tile(s):
    return jnp.maximum(s - 1, 0)

  in_specs = [_whole(pltpu.SMEM),
              pl.BlockSpec((1, tq, D_MODEL), lambda s: (mixer_tile(s) // n_t, mixer_tile(s) % n_t, 0))]
  in_specs += [vm] * len(weights)
  out_shape = (
      jax.ShapeDtypeStruct((batch, seq, D_MODEL), F32),
      jax.ShapeDtypeStruct((batch, WINDOW, KV_W), F32),
      jax.ShapeDtypeStruct((batch, WINDOW, KV_W), F32),
      jax.ShapeDtypeStruct((batch, CONV_W - 1, LRU_W), F32),
      jax.ShapeDtypeStruct((batch, 1, LRU_W), F32),
  )
  state_map = lambda s: (mixer_tile(s) // n_t, 0, 0)
  out_specs = (
      pl.BlockSpec((1, tq, D_MODEL), lambda s: (ffn_tile(s) // n_t, ffn_tile(s) % n_t, 0)),
      pl.BlockSpec((1, WINDOW, KV_W), state_map),
      pl.BlockSpec((1, WINDOW, KV_W), state_map),
      pl.BlockSpec((1, CONV_W - 1, LRU_W), state_map),
      pl.BlockSpec((1, 1, LRU_W), state_map),
  )
  scratch = [
      pltpu.VMEM((tq, ATTN_W), BF16),
      pltpu.VMEM((4, WINDOW + tq, KV_W), BF16),
      pltpu.VMEM((4, WINDOW + tq, KV_W), BF16),
      pltpu.VMEM((LRU_W // LANES, SUBLANES + tq, LANES), F32),
      pltpu.VMEM((LRU_W // LANES, tq, LANES), F32),
      pltpu.VMEM((LRU_ROWS // SUBLANES, LRU_W), F32),
      pltpu.VMEM((SUBLANES, LRU_W), F32),
      pltpu.VMEM((tq, ATTN_W), BF16),
      pltpu.VMEM((LRU_W // LANES, tq, LANES), F32),
      pltpu.VMEM((2, tq, D_MODEL), F32),
      pltpu.VMEM((tq, D_MODEL), BF16),
      pltpu.VMEM((tq, D_FF), BF16),
      pltpu.VMEM((N_HEADS, WINDOW, 2 * WINDOW), F32),
  ]
  return pl.pallas_call(
      functools.partial(_prompt_kernel, tiles_per_seq=n_t),
      grid=(n_tiles + 1,),
      in_specs=in_specs,
      out_specs=out_specs,
      out_shape=out_shape,
      scratch_shapes=scratch,
      compiler_params=pltpu.CompilerParams(
          dimension_semantics=("arbitrary",),
          vmem_limit_bytes=VMEM_LIMIT_BYTES),
      name="prompt_layer",
  )(sinks, x, *weights)


def _sample_kernel(
    sink_ref, x_ref, ck_ref, cv_ref, cpad_ref, h0_ref, gmix_ref, win_ref, bin_ref, convw_ref,
    convb_ref, wa_ref, ba_ref, wi_ref, bi_ref, lam_ref, wout_ref, bout_ref, gffn_ref, wfi_ref,
    wfo_ref, gfin_ref,
    y_ref, nk_ref, nv_ref, xb_ref, hs_ref,
    q_s, kn_s, vn_s, yb_s, attn_s, act_s):
  n_tok = CONV_W
  rows = x_ref.shape[0]
  step = pl.program_id(0)

  @pl.when(step == 0)
  def _():
    u = _rmsnorm(x_ref[...], gmix_ref[...]).astype(BF16)
    q_s[...] = (_dot(u, win_ref[:, 0:Q_END]) + bin_ref[:, 0:Q_END]) * (HEAD_DIM ** -0.5)
    kv = _dot(u, win_ref[:, Q_END:V_END]) + bin_ref[:, Q_END:V_END]
    kn_s[...] = kv[:, 0:KV_W]
    vn_s[...] = kv[:, KV_W:2 * KV_W]
    xb_ref[...] = _dot(u, win_ref[:, V_END:X_END]) + bin_ref[:, V_END:X_END]
    yb_s[...] = _dot(u, win_ref[:, X_END:IN_W]) + bin_ref[:, X_END:IN_W]

  prow = lax.broadcasted_iota(jnp.int32, (PAIR_ROWS, 2 * WINDOW), 0)
  pcol = lax.broadcasted_iota(jnp.int32, (PAIR_ROWS, 2 * WINDOW), 1)
  seq_r = (prow % SUBLANES) // n_tok
  tok_r = prow % n_tok
  dist_c = WINDOW + tok_r - (pcol % WINDOW)
  valid_c = (seq_r == pcol // WINDOW) & (dist_c >= 0) & (dist_c < WINDOW)
  nrow = lax.broadcasted_iota(jnp.int32, (PAIR_ROWS, LANES), 0)
  ncol = lax.broadcasted_iota(jnp.int32, (PAIR_ROWS, LANES), 1)
  dist_n = (nrow % n_tok) - (ncol % n_tok)
  valid_n = (ncol < SUBLANES) & ((nrow % SUBLANES) // n_tok == ncol // n_tok) & (dist_n >= 0)
  slope_c = jnp.zeros((PAIR_ROWS, 2 * WINDOW), F32)
  slope_n = jnp.zeros((PAIR_ROWS, LANES), F32)
  for h in range(N_HEADS):
    slope_c = jnp.where(prow // SUBLANES == h, _alibi_slope(h), slope_c)
    slope_n = jnp.where(nrow // SUBLANES == h, _alibi_slope(h), slope_n)
  bias_c = slope_c * dist_c.astype(F32)
  bias_n = slope_n * dist_n.astype(F32)
  sink = sink_ref[...]
  pad = jnp.zeros((LANES - SUBLANES, KV_W), F32)
  low_half = lax.broadcasted_iota(jnp.int32, (SUBLANES, KV_W), 1) < HEAD_DIM
  first_row = step * (SAMPLE_SEQS * n_tok)
  kept = WINDOW - n_tok

  def scores(c):
    r0 = pl.multiple_of(first_row + c * SUBLANES, SUBLANES)
    blocks = []
    for h in range(N_HEADS):
      grp = q_s[pl.ds(r0, SUBLANES), (h // 2) * KV_W:(h // 2 + 1) * KV_W]
      to_low = h // GROUP == 0
      if (h % 2 == 0) != to_low:
        grp = pltpu.roll(grp, HEAD_DIM, axis=1)
      blocks.append(jnp.where(low_half == to_low, grp, 0.0))
    lhs = jnp.concatenate(blocks, axis=0).astype(BF16)
    kn = kn_s[pl.ds(r0, SUBLANES), :]
    kc = ck_ref[pl.ds(2 * c, 2)].reshape(2 * WINDOW, KV_W).astype(BF16)
    s_c = jnp.where(valid_c, _dot_t(lhs, kc) - bias_c, NEG)
    s_n = jnp.where(
        valid_n, _dot_t(lhs, jnp.concatenate([kn, pad], axis=0).astype(BF16)) - bias_n, NEG)
    return s_c, s_n

  def softmax(s_c, s_n):
    m = jnp.maximum(
        jnp.maximum(jnp.max(s_c, axis=-1, keepdims=True), jnp.max(s_n, axis=-1, keepdims=True)),
        sink)
    e_c = jnp.exp(s_c - m)
    e_n = jnp.exp(s_n - m)
    denom = (jnp.sum(e_c, axis=-1, keepdims=True) + jnp.sum(e_n, axis=-1, keepdims=True)
             + jnp.exp(sink - m))
    return e_c.astype(BF16), e_n.astype(BF16), 1.0 / denom

  def weighted_values(c, e_c, e_n, inv):
    r0 = pl.multiple_of(first_row + c * SUBLANES, SUBLANES)
    vn = vn_s[pl.ds(r0, SUBLANES), :]
    vc = cv_ref[pl.ds(2 * c, 2)].reshape(2 * WINDOW, KV_W).astype(BF16)
    o = (_dot(e_c, vc) + _dot(e_n, jnp.concatenate([vn, pad], axis=0).astype(BF16))) * inv
    for hp in range(N_PAIRS):
      lo = o[2 * hp * SUBLANES:(2 * hp + 1) * SUBLANES, :]
      hi = o[(2 * hp + 1) * SUBLANES:(2 * hp + 2) * SUBLANES, :]
      if (2 * hp) // GROUP == 0:
        hi = pltpu.roll(hi, HEAD_DIM, axis=1)
      else:
        lo = pltpu.roll(lo, HEAD_DIM, axis=1)
      attn_s[pl.ds(r0, SUBLANES), hp * KV_W:(hp + 1) * KV_W] = jnp.where(low_half, lo, hi)

  def shift_caches(c):
    r0 = pl.multiple_of(first_row + c * SUBLANES, SUBLANES)
    for new_ref, cache_ref, out_ref in ((kn_s, ck_ref, nk_ref), (vn_s, cv_ref, nv_ref)):
      out_ref[pl.ds(2 * c, 2), 0:kept, :] = cache_ref[pl.ds(2 * c, 2), n_tok:WINDOW, :]
      out_ref[pl.ds(2 * c, 1), kept:WINDOW, :] = new_ref[pl.ds(r0, n_tok), :].reshape(1, n_tok, KV_W)
      out_ref[pl.ds(2 * c + 1, 1), kept:WINDOW, :] = (
          new_ref[pl.ds(r0 + n_tok, n_tok), :].reshape(1, n_tok, KV_W))

  def pair_group(gi, carry):
    cs = [gi * SAMPLE_PAIRS_PER_ITER + i for i in range(SAMPLE_PAIRS_PER_ITER)]
    raw = [scores(c) for c in cs]
    probs = [softmax(*s) for s in raw]
    for c, pr in zip(cs, probs):
      weighted_values(c, *pr)
      shift_caches(c)
    return carry

  lax.fori_loop(0, SAMPLE_SEQS // 2 // SAMPLE_PAIRS_PER_ITER, pair_group, 0)

  @pl.when(step == pl.num_programs(0) - 1)
  def _():
    x = x_ref[...]
    xb = xb_ref[...]
    tok = lax.broadcasted_iota(jnp.int32, (rows, LRU_W), 0) % n_tok
    cpad = cpad_ref[...]
    xc = convb_ref[...] + convw_ref[CONV_W - 1:CONV_W, :] * xb
    for shift in range(1, CONV_W):
      saved = cpad if shift == CONV_W - 1 else pltpu.roll(cpad, rows - (CONV_W - 1 - shift), axis=0)
      prev = jnp.where(tok >= shift, pltpu.roll(xb, shift, axis=0), saved)
      xc = xc + convw_ref[CONV_W - 1 - shift:CONV_W - shift, :] * prev

    a, b = _lru_coeffs(xc, wa_ref, ba_ref, wi_ref, bi_ref, lam_ref)
    a, b = _scan_within_groups(a, b, n_tok)
    hs = a * h0_ref[...] + b
    hs_ref[...] = hs
    lru = hs * jax.nn.gelu(yb_s[...])

    hres = (x + _dot(attn_s[...].astype(BF16), wout_ref[0:ATTN_W, :])
            + _dot(lru.astype(BF16), wout_ref[ATTN_W:ATTN_W + LRU_W, :]) + bout_ref[...])
    hn = _rmsnorm(hres, gffn_ref[...]).astype(BF16)

    def write_output(first_row, out_rows):
      y_ref[first_row:first_row + out_rows.shape[0], :] = out_rows

    _Ffn(hres, hn, wfi_ref, wfo_ref, gfin_ref, act_s, write_output).finish()


def _sample_call(x, ck, cv, cpad, h0rep, sink_rows, weights):
  n_rows = x.shape[0]
  n_seqs = ck.shape[0]
  vm = _whole(pltpu.VMEM)
  cache_spec = pl.BlockSpec((SAMPLE_SEQS, WINDOW, KV_W), lambda i: (i, 0, 0))
  in_specs = [vm, vm, cache_spec, cache_spec, vm, vm] + [vm] * len(weights)
  out_shape = (
      jax.ShapeDtypeStruct((n_rows, D_MODEL), F32),
      jax.ShapeDtypeStruct((n_seqs, WINDOW, KV_W), F32),
      jax.ShapeDtypeStruct((n_seqs, WINDOW, KV_W), F32),
      jax.ShapeDtypeStruct((n_rows, LRU_W), F32),
      jax.ShapeDtypeStruct((n_rows, LRU_W), F32),
  )
  out_specs = (vm, cache_spec, cache_spec, vm, vm)
  scratch = [
      pltpu.VMEM((n_rows, ATTN_W), F32),
      pltpu.VMEM((n_rows, KV_W), F32),
      pltpu.VMEM((n_rows, KV_W), F32),
      pltpu.VMEM((n_rows, LRU_W), F32),
      pltpu.VMEM((n_rows, ATTN_W), F32),
      pltpu.VMEM((n_rows, D_FF), BF16),
  ]
  return pl.pallas_call(
      _sample_kernel,
      grid=(n_seqs // SAMPLE_SEQS,),
      in_specs=in_specs,
      out_specs=out_specs,
      out_shape=out_shape,
      scratch_shapes=scratch,
      compiler_params=pltpu.CompilerParams(
          dimension_semantics=("arbitrary",),
          vmem_limit_bytes=VMEM_LIMIT_BYTES),
      name="sample_layer",
  )(sink_rows, x, ck, cv, cpad, h0rep, *weights)


def _block_diag_halves(w):
  per = LRU_BLOCKS // 2
  blocks = w.reshape(2, per, 1, LRU_BW, LRU_BW)
  on_diag = jnp.eye(per, dtype=bool).reshape(1, per, per, 1, 1)
  tiles = jnp.where(on_diag, blocks, 0.0)
  return tiles.transpose(0, 1, 3, 2, 4).reshape(2, per * LRU_BW, per * LRU_BW).astype(BF16)


def kernel(x_prompt, x_sample, cache_k, cache_v, state_conv, state_h, g_mix, w_in, b_in,
           attn_sinks, conv_w, conv_b, w_a, b_a, w_i, b_i, lam, w_out, b_out, g_ffn,
           w_ffn_in, w_ffn_out, g_final):
  depth = g_mix.shape[0]
  assert depth == 1
  batch, seq, _ = x_prompt.shape
  dec_batch, dec_seq, _ = x_sample.shape
  assert seq % PROMPT_TILE == 0 and dec_seq == CONV_W and dec_batch % SAMPLE_SEQS == 0

  row = lambda p: p.reshape(1, -1)
  weights = [row(g_mix[0]), w_in[0].astype(BF16), row(b_in[0]), conv_w[0], row(conv_b[0]),
             _block_diag_halves(w_a[0]), row(b_a[0]), _block_diag_halves(w_i[0]), row(b_i[0]),
             row(lam[0]), w_out[0].astype(BF16), row(b_out[0]), row(g_ffn[0]),
             w_ffn_in[0].astype(BF16), w_ffn_out[0].astype(BF16), row(g_final)]

  y_p, nk_p, nv_p, nconv_p, nh_p = _prompt_call(x_prompt, attn_sinks[0], weights)

  n_rows = dec_batch * dec_seq
  cpad = jnp.pad(state_conv[0], ((0, 0), (0, 1), (0, 0))).reshape(n_rows, LRU_W)
  h0rep = jnp.repeat(state_h[0], dec_seq, axis=0)
  sink_rows = jnp.repeat(attn_sinks[0], SUBLANES).reshape(PAIR_ROWS, 1)
  y_s, nk_s, nv_s, xb_s, hs_s = _sample_call(
      x_sample.reshape(n_rows, D_MODEL), cache_k[0].reshape(dec_batch, WINDOW, KV_W),
      cache_v[0].reshape(dec_batch, WINDOW, KV_W), cpad, h0rep, sink_rows, weights)
  nconv_s = xb_s.reshape(dec_batch, dec_seq, LRU_W)[:, dec_seq - (CONV_W - 1):]
  nh_s = hs_s.reshape(dec_batch, dec_seq, LRU_W)[:, dec_seq - 1]

  pkv = (1, batch, WINDOW, N_KV_HEADS, HEAD_DIM)
  skv = (1, dec_batch, WINDOW, N_KV_HEADS, HEAD_DIM)
  return (y_p, y_s.reshape(dec_batch, dec_seq, D_MODEL),
          nk_p.reshape(pkv), nv_p.reshape(pkv), nconv_p[None], nh_p.reshape(1, batch, LRU_W),
          nk_s.reshape(skv), nv_s.reshape(skv), nconv_s[None], nh_s[None])
```

```python
import functools

import jax
import jax.numpy as jnp
from jax import lax
from jax.experimental import pallas as pl
from jax.experimental.pallas import tpu as pltpu

D_MODEL = 1024
N_HEADS = 8
N_KV_HEADS = 2
HEAD_DIM = 64
GROUP = N_HEADS // N_KV_HEADS
ATTN_W = N_HEADS * HEAD_DIM
KV_W = N_KV_HEADS * HEAD_DIM
WINDOW = 128
LRU_W = D_MODEL // 2
LRU_BLOCKS = 8
LRU_BW = LRU_W // LRU_BLOCKS
CONV_W = 4
LRU_C = 8.0
IN_W = ATTN_W + 2 * KV_W + 2 * LRU_W
Q_END = ATTN_W
K_END = Q_END + KV_W
V_END = K_END + KV_W
X_END = V_END + LRU_W
D_FF = 2816
EPS = 1e-6
NEG = -1e30
LOG2E = 1.4426950408889634

SUBLANES = 8
LANES = 128
VMEM_LIMIT_BYTES = 56 * 1024 * 1024

PROMPT_TILE = 512
FFN_CHUNK = 256
DOWN_ROWS = 256
LRU_ROWS = 64
N_PAIRS = N_HEADS // 2
SAMPLE_SEQS = 16
PAIR_ROWS = N_HEADS * SUBLANES
SAMPLE_PAIRS_PER_ITER = 8

F32 = jnp.float32
BF16 = jnp.bfloat16


def _dot(a, b):
  return jnp.dot(a, b, preferred_element_type=F32)


def _dot_t(a, b):
  return lax.dot_general(a, b, (((1,), (1,)), ((), ())), preferred_element_type=F32)


def _rmsnorm(x, g):
  return x * lax.rsqrt(jnp.mean(x * x, axis=-1, keepdims=True) + EPS) * g


def _alibi_slope(h):
  return 2.0 ** (-8.0 * (h + 1) / N_HEADS)


def _lru_coeffs(xc, wa_ref, ba_ref, wi_ref, bi_ref, lam_ref):
  xcb = xc.astype(BF16)
  half = LRU_W // 2
  lo, hi = xcb[:, :half], xcb[:, half:]
  r = jnp.concatenate([_dot(lo, wa_ref[0]), _dot(hi, wa_ref[1])], axis=-1) + ba_ref[...]
  ig = jnp.concatenate([_dot(lo, wi_ref[0]), _dot(hi, wi_ref[1])], axis=-1) + bi_ref[...]
  r = jax.nn.sigmoid(r)
  ig = jax.nn.sigmoid(ig)
  nlam = -lam_ref[...]
  softplus = jnp.maximum(nlam, 0.0) + jnp.log1p(jnp.exp(-jnp.abs(nlam)))
  log_a = r * (-LRU_C * softplus)
  a = jnp.exp(log_a)
  gain_sq = -jnp.tanh(log_a) * (a * a + 1.0)
  gain = jnp.where(gain_sq > 0.0, gain_sq * lax.rsqrt(gain_sq), 0.0)
  return a, gain * (ig * xc)


def _scan_within_groups(a, b, period):
  t, w = a.shape
  a3 = a.reshape(t // SUBLANES, SUBLANES, w)
  b3 = b.reshape(t // SUBLANES, SUBLANES, w)
  pos = lax.broadcasted_iota(jnp.int32, (1, SUBLANES, w), 1) % period
  d = 1
  while d < period:
    keep = pos >= d
    a_prev = jnp.where(keep, pltpu.roll(a3, d, axis=1), 1.0)
    b_prev = jnp.where(keep, pltpu.roll(b3, d, axis=1), 0.0)
    b3 = a3 * b_prev + b3
    a3 = a3 * a_prev
    d *= 2
  return a3.reshape(t, w), b3.reshape(t, w)


class _Ffn:
  def __init__(self, n_rows, read_hres, read_hn, wfi_ref, wfo_ref, gfin_ref, act_ref, write):
    self.read_hres, self.read_hn, self.write = read_hres, read_hn, write
    self.wfi_ref, self.wfo_ref, self.gfin_ref, self.act_ref = wfi_ref, wfo_ref, gfin_ref, act_ref
    self.n_up = D_FF // FFN_CHUNK
    self.down_rows = min(DOWN_ROWS, n_rows)
    self.n_units = self.n_up + n_rows // self.down_rows
    self.done = 0

  def run(self, n_units=1):
    for _ in range(min(n_units, self.n_units - self.done)):
      if self.done < self.n_up:
        lo = self.done * FFN_CHUNK
        w = jnp.concatenate([self.wfi_ref[:, lo:lo + FFN_CHUNK],
                             self.wfi_ref[:, D_FF + lo:D_FF + lo + FFN_CHUNK]], axis=-1)
        gate_up = _dot(self.read_hn(), w)
        gate, up = gate_up[:, :FFN_CHUNK], gate_up[:, FFN_CHUNK:]
        self.act_ref[:, lo:lo + FFN_CHUNK] = (jax.nn.silu(gate) * up).astype(BF16)
      else:
        r0 = (self.done - self.n_up) * self.down_rows
        r1 = r0 + self.down_rows
        out = self.read_hres(r0, r1) + _dot(self.act_ref[r0:r1, :], self.wfo_ref[...])
        self.write(r0, _rmsnorm(out, self.gfin_ref[...]))
      self.done += 1

  def finish(self):
    self.run(self.n_units)


def _prompt_kernel(
    sinks_ref, x_ref, gmix_ref, win_ref, bin_ref, convw_ref, convb_ref, wa_ref, ba_ref,
    wi_ref, bi_ref, lam_ref, wout_ref, bout_ref, gffn_ref, wfi_ref, wfo_ref, gfin_ref,
    y_ref, nk_ref, nv_ref, nconv_ref, nh_ref,
    q_s, kz_s, vz_s, xpad_s, yb_s, hin_s, hcar_s, attn_s, lru_s, hres_s, hn_s, act_s, bias_s, *,
    tiles_per_seq):
  tq = PROMPT_TILE
  step = pl.program_id(0)
  slot = step % 2
  t_idx = step % tiles_per_seq
  first_tile = t_idx == 0
  last_tile = t_idx == tiles_per_seq - 1

  row = lax.broadcasted_iota(jnp.int32, (WINDOW, 2 * WINDOW), 0)
  col = lax.broadcasted_iota(jnp.int32, (WINDOW, 2 * WINDOW), 1)
  dist = WINDOW + row - col

  @pl.when(step == 0)
  def _():
    hres_s[...] = jnp.zeros((2, tq, D_MODEL), F32)
    hn_s[...] = jnp.zeros((tq, D_MODEL), BF16)
    for h in range(N_HEADS):
      bias_s[h] = (_alibi_slope(h) * LOG2E) * dist.astype(F32)

  @pl.when(first_tile)
  def _():
    kz_s[:, 0:WINDOW, :] = jnp.zeros((4, WINDOW, KV_W), BF16)
    vz_s[:, 0:WINDOW, :] = jnp.zeros((4, WINDOW, KV_W), BF16)
    xpad_s[:, 0:SUBLANES, :] = jnp.zeros((LRU_W // LANES, SUBLANES, LANES), F32)
    hcar_s[...] = jnp.zeros((SUBLANES, LRU_W), F32)

  def write_output(first_row, rows):
    y_ref[0, first_row:first_row + rows.shape[0], :] = rows

  def read_previous_hres(r0, r1):
    return hres_s[1 - slot, r0:r1, :]

  def read_previous_hn():
    return hn_s[...]

  ffn = _Ffn(tq, read_previous_hres, read_previous_hn, wfi_ref, wfo_ref, gfin_ref, act_s,
             write_output)
  ffn.run(1)

  x = x_ref[0]
  u = _rmsnorm(x, gmix_ref[...]).astype(BF16)

  q = _dot(u, win_ref[:, 0:Q_END]) + bin_ref[:, 0:Q_END]
  q_s[...] = (q * (HEAD_DIM ** -0.5 * LOG2E)).astype(BF16)
  w_kv = win_ref[:, Q_END:V_END]
  kv = jnp.concatenate([_dot(u[:tq // 2], w_kv), _dot(u[tq // 2:], w_kv)], axis=0)
  kv = kv + bin_ref[:, Q_END:V_END]
  k = kv[:, 0:KV_W]
  v = kv[:, KV_W:2 * KV_W]
  nk_ref[0] = k[tq - WINDOW:tq, :]
  nv_ref[0] = v[tq - WINDOW:tq, :]
  lru_in = {}

  def project_lru_section(name, lo):
    lru_in[name] = _dot(u, win_ref[:, lo:lo + LRU_W]) + bin_ref[:, lo:lo + LRU_W]
    if name == "xb":
      nconv_ref[0] = lru_in[name][tq - (CONV_W - 1):tq, :]

  half = tq // 2
  out_partial = {}

  def project_attention(rows):
    out_partial[rows] = _dot(attn_s[rows * half:(rows + 1) * half, :], wout_ref[0:ATTN_W, :])

  def finish_mixer(rows):
    r0 = rows * half
    lru = jnp.concatenate([lru_s[g, r0:r0 + half, :] for g in range(LRU_W // LANES)], axis=-1)
    hres = (x_ref[0, r0:r0 + half, :] + out_partial[rows]
            + _dot(lru.astype(BF16), wout_ref[ATTN_W:ATTN_W + LRU_W, :]) + bout_ref[...])
    hres_s[slot, r0:r0 + half, :] = hres
    hn_s[r0:r0 + half, :] = _rmsnorm(hres, gffn_ref[...]).astype(BF16)

  n_problems = (tq // WINDOW) * N_PAIRS
  fixed_fillers = {0: functools.partial(project_lru_section, "xb", V_END),
                   1: functools.partial(project_lru_section, "yb", X_END),
                   n_problems - 3: functools.partial(project_attention, 0),
                   n_problems - 2: functools.partial(finish_mixer, 0)}

  def filler(i):
    if i in fixed_fillers:
      fixed_fillers[i]()
    else:
      ffn.run(1)

  lane = lax.broadcasted_iota(jnp.int32, (tq, KV_W), 1)
  is_lo = lane < HEAD_DIM
  for src, dst in ((k, kz_s), (v, vz_s)):
    swapped = pltpu.roll(src, HEAD_DIM, axis=1)
    dst[0, WINDOW:WINDOW + tq, :] = jnp.where(is_lo, src, 0.0).astype(BF16)
    dst[1, WINDOW:WINDOW + tq, :] = jnp.where(is_lo, 0.0, swapped).astype(BF16)
    dst[2, WINDOW:WINDOW + tq, :] = jnp.where(is_lo, swapped, 0.0).astype(BF16)
    dst[3, WINDOW:WINDOW + tq, :] = jnp.where(is_lo, 0.0, src).astype(BF16)

  key_rank = jnp.where(col >= WINDOW, jnp.where(dist >= 0, -1, 2 * WINDOW), dist)
  lane_q = lax.broadcasted_iota(jnp.int32, (WINDOW, KV_W), 1)

  def operand(ref, j, p):
    g = p // (N_PAIRS // N_KV_HEADS)
    r0 = j * WINDOW
    return jnp.concatenate(
        [ref[2 * g, r0:r0 + 2 * WINDOW, :], ref[2 * g + 1, r0:r0 + 2 * WINDOW, :]], axis=0)

  def scores(j, p):
    r0 = j * WINDOW
    return _dot_t(q_s[r0:r0 + WINDOW, p * KV_W:(p + 1) * KV_W], operand(kz_s, j, p))

  def softmax(s, j, p):
    limit = jnp.where(first_tile, 0, WINDOW) if j == 0 else WINDOW
    valid = key_rank < limit
    probs, inv = [], []
    for hh in range(2):
      h = 2 * p + hh
      sink = sinks_ref[h] * LOG2E
      sh = s[:, hh * 2 * WINDOW:(hh + 1) * 2 * WINDOW]
      sh = jnp.where(valid, sh - bias_s[h], NEG)
      m = jnp.maximum(jnp.max(sh, axis=-1, keepdims=True), sink)
      e = jnp.exp2(sh - m)
      denom = jnp.sum(e, axis=-1, keepdims=True) + jnp.exp2(sink - m)
      probs.append(e.astype(BF16))
      inv.append(1.0 / denom)
    return jnp.concatenate(probs, axis=-1), jnp.where(lane_q < HEAD_DIM, inv[0], inv[1])

  def weighted_values(j, p, probs, inv):
    r0 = j * WINDOW
    o = _dot(probs, operand(vz_s, j, p)) * inv
    attn_s[r0:r0 + WINDOW, p * KV_W:(p + 1) * KV_W] = o.astype(BF16)

  lru_state = {}
  n_groups = LRU_ROWS // SUBLANES

  lane_groups = LRU_W // LANES

  def put_rows(ref, first_row, val):
    for g in range(lane_groups):
      ref[g, first_row:first_row + val.shape[0], :] = val[:, g * LANES:(g + 1) * LANES]

  def phase_view(ref, first_row):
    return jnp.concatenate(
        [ref[g, pl.ds(first_row, n_groups, stride=SUBLANES), :] for g in range(lane_groups)], axis=-1)

  def phase_store(ref, first_row, val):
    for g in range(lane_groups):
      ref[g, pl.ds(first_row, n_groups, stride=SUBLANES), :] = val[:, g * LANES:(g + 1) * LANES]

  def lru_conv(rc):
    if rc == 0:
      put_rows(xpad_s, SUBLANES, lru_in["xb"])
      lru_state["h"] = hcar_s[0:1, :]
    c0 = SUBLANES + rc * LRU_ROWS
    shifted = {d: phase_view(xpad_s, c0 + d) for d in range(1 - CONV_W, SUBLANES)}
    phases = []
    for r in range(SUBLANES):
      xc = convb_ref[...] + convw_ref[CONV_W - 1:CONV_W, :] * shifted[r]
      for kk in range(CONV_W - 1):
        xc = xc + convw_ref[kk:kk + 1, :] * shifted[r - (CONV_W - 1) + kk]
      phases.append(xc)
    lru_state["xc"] = jnp.concatenate(phases, axis=0)

  def lru_recurrence(rc):
    if rc == 0:
      put_rows(yb_s, 0, lru_in["yb"])
    c0 = rc * LRU_ROWS
    a, b = _lru_coeffs(lru_state["xc"], wa_ref, ba_ref, wi_ref, bi_ref, lam_ref)
    phase = lambda t, r: t[r * n_groups:(r + 1) * n_groups, :]
    a_cum, b_cum = [phase(a, 0)], [phase(b, 0)]
    for r in range(1, SUBLANES):
      b_cum.append(phase(a, r) * b_cum[-1] + phase(b, r))
      a_cum.append(phase(a, r) * a_cum[-1])
    h = lru_state["h"]
    for gi in range(n_groups):
      hin_s[gi:gi + 1, :] = h
      h = a_cum[-1][gi:gi + 1, :] * h + b_cum[-1][gi:gi + 1, :]
    lru_state["h"] = h
    h_in = hin_s[...]
    for r in range(SUBLANES):
      hs = a_cum[r] * h_in + b_cum[r]
      gate = jax.nn.gelu(phase_view(yb_s, c0 + r))
      phase_store(lru_s, c0 + r, hs * gate)

  problems = [(j, p) for j in range(tq // WINDOW) for p in range(N_PAIRS)]
  lru_period = len(problems) // (tq // LRU_ROWS)
  raw = {i: scores(*problems[i]) for i in range(2)}
  ready = {}
  for i, (j, p) in enumerate(problems):
    if i + 2 < len(problems):
      raw[i + 2] = scores(*problems[i + 2])
    ready[i] = softmax(raw.pop(i), j, p)
    filler(i)
    if i % lru_period == 1:
      lru_recurrence(i // lru_period)
    if i >= 1:
      weighted_values(*problems[i - 1], *ready.pop(i - 1))
    if i % lru_period == 0:
      lru_conv(i // lru_period)
  weighted_values(*problems[-1], *ready.pop(len(problems) - 1))

  kz_s[:, 0:WINDOW, :] = kz_s[:, tq:tq + WINDOW, :]
  vz_s[:, 0:WINDOW, :] = vz_s[:, tq:tq + WINDOW, :]
  h = lru_state["h"]
  xpad_s[:, 0:SUBLANES, :] = xpad_s[:, tq:tq + SUBLANES, :]
  hcar_s[...] = jnp.broadcast_to(h, (SUBLANES, LRU_W))

  @pl.when(last_tile)
  def _():
    nh_ref[0] = h

  project_attention(1)
  finish_mixer(1)
  ffn.finish()


def _whole(space):
  return pl.BlockSpec(memory_space=space)


def _prompt_call(x, sinks, weights):
  batch, seq, _ = x.shape
  tq = PROMPT_TILE
  n_t = seq // tq
  n_tiles = batch * n_t
  vm = _whole(pltpu.VMEM)

  def mixer_tile(s):
    return jnp.minimum(s, n_tiles - 1)

  def ffn_tile(s):
    return jnp.maximum(s - 1, 0)

  in_specs = [_whole(pltpu.SMEM),
              pl.BlockSpec((1, tq, D_MODEL), lambda s: (mixer_tile(s) // n_t, mixer_tile(s) % n_t, 0))]
  in_specs += [vm] * len(weights)
  out_shape = (
      jax.ShapeDtypeStruct((batch, seq, D_MODEL), F32),
      jax.ShapeDtypeStruct((batch, WINDOW, KV_W), F32),
      jax.ShapeDtypeStruct((batch, WINDOW, KV_W), F32),
      jax.ShapeDtypeStruct((batch, CONV_W - 1, LRU_W), F32),
      jax.ShapeDtypeStruct((batch, 1, LRU_W), F32),
  )
  state_map = lambda s: (mixer_tile(s) // n_t, 0, 0)
  out_specs = (
      pl.BlockSpec((1, tq, D_MODEL), lambda s: (ffn_tile(s) // n_t, ffn_tile(s) % n_t, 0)),
      pl.BlockSpec((1, WINDOW, KV_W), state_map),
      pl.BlockSpec((1, WINDOW, KV_W), state_map),
      pl.BlockSpec((1, CONV_W - 1, LRU_W), state_map),
      pl.BlockSpec((1, 1, LRU_W), state_map),
  )
  scratch = [
      pltpu.VMEM((tq, ATTN_W), BF16),
      pltpu.VMEM((4, WINDOW + tq, KV_W), BF16),
      pltpu.VMEM((4, WINDOW + tq, KV_W), BF16),
      pltpu.VMEM((LRU_W // LANES, SUBLANES + tq, LANES), F32),
      pltpu.VMEM((LRU_W // LANES, tq, LANES), F32),
      pltpu.VMEM((LRU_ROWS // SUBLANES, LRU_W), F32),
      pltpu.VMEM((SUBLANES, LRU_W), F32),
      pltpu.VMEM((tq, ATTN_W), BF16),
      pltpu.VMEM((LRU_W // LANES, tq, LANES), F32),
      pltpu.VMEM((2, tq, D_MODEL), F32),
      pltpu.VMEM((tq, D_MODEL), BF16),
      pltpu.VMEM((tq, D_FF), BF16),
      pltpu.VMEM((N_HEADS, WINDOW, 2 * WINDOW), F32),
  ]
  return pl.pallas_call(
      functools.partial(_prompt_kernel, tiles_per_seq=n_t),
      grid=(n_tiles + 1,),
      in_specs=in_specs,
      out_specs=out_specs,
      out_shape=out_shape,
      scratch_shapes=scratch,
      compiler_params=pltpu.CompilerParams(
          dimension_semantics=("arbitrary",),
          vmem_limit_bytes=VMEM_LIMIT_BYTES),
      name="prompt_layer",
  )(sinks, x, *weights)


def _sample_kernel(
    sink_ref, x_ref, ck_ref, cv_ref, cpad_ref, h0_ref, gmix_ref, win_ref, bin_ref, convw_ref,
    convb_ref, wa_ref, ba_ref, wi_ref, bi_ref, lam_ref, wout_ref, bout_ref, gffn_ref, wfi_ref,
    wfo_ref, gfin_ref,
    y_ref, nk_ref, nv_ref, xb_ref, hs_ref,
    q_s, kn_s, vn_s, yb_s, attn_s, act_s):
  n_tok = CONV_W
  rows = x_ref.shape[0]
  step = pl.program_id(0)

  @pl.when(step == 0)
  def _():
    u = _rmsnorm(x_ref[...], gmix_ref[...]).astype(BF16)
    q_s[...] = (_dot(u, win_ref[:, 0:Q_END]) + bin_ref[:, 0:Q_END]) * (HEAD_DIM ** -0.5)
    kv = _dot(u, win_ref[:, Q_END:V_END]) + bin_ref[:, Q_END:V_END]
    kn_s[...] = kv[:, 0:KV_W]
    vn_s[...] = kv[:, KV_W:2 * KV_W]
    xb_ref[...] = _dot(u, win_ref[:, V_END:X_END]) + bin_ref[:, V_END:X_END]
    yb_s[...] = _dot(u, win_ref[:, X_END:IN_W]) + bin_ref[:, X_END:IN_W]

  prow = lax.broadcasted_iota(jnp.int32, (PAIR_ROWS, 2 * WINDOW), 0)
  pcol = lax.broadcasted_iota(jnp.int32, (PAIR_ROWS, 2 * WINDOW), 1)
  seq_r = (prow % SUBLANES) // n_tok
  tok_r = prow % n_tok
  dist_c = WINDOW + tok_r - (pcol % WINDOW)
  valid_c = (seq_r == pcol // WINDOW) & (dist_c >= 0) & (dist_c < WINDOW)
  nrow = lax.broadcasted_iota(jnp.int32, (PAIR_ROWS, LANES), 0)
  ncol = lax.broadcasted_iota(jnp.int32, (PAIR_ROWS, LANES), 1)
  dist_n = (nrow % n_tok) - (ncol % n_tok)
  valid_n = (ncol < SUBLANES) & ((nrow % SUBLANES) // n_tok == ncol // n_tok) & (dist_n >= 0)
  slope_c = jnp.zeros((PAIR_ROWS, 2 * WINDOW), F32)
  slope_n = jnp.zeros((PAIR_ROWS, LANES), F32)
  for h in range(N_HEADS):
    slope_c = jnp.where(prow // SUBLANES == h, _alibi_slope(h), slope_c)
    slope_n = jnp.where(nrow // SUBLANES == h, _alibi_slope(h), slope_n)
  bias_c = slope_c * dist_c.astype(F32)
  bias_n = slope_n * dist_n.astype(F32)
  sink = sink_ref[...]
  pad = jnp.zeros((LANES - SUBLANES, KV_W), F32)
  low_half = lax.broadcasted_iota(jnp.int32, (SUBLANES, KV_W), 1) < HEAD_DIM
  first_row = step * (SAMPLE_SEQS * n_tok)
  kept = WINDOW - n_tok

  def scores(c):
    r0 = pl.multiple_of(first_row + c * SUBLANES, SUBLANES)
    blocks = []
    for h in range(N_HEADS):
      grp = q_s[pl.ds(r0, SUBLANES), (h // 2) * KV_W:(h // 2 + 1) * KV_W]
      to_low = h // GROUP == 0
      if (h % 2 == 0) != to_low:
        grp = pltpu.roll(grp, HEAD_DIM, axis=1)
      blocks.append(jnp.where(low_half == to_low, grp, 0.0))
    lhs = jnp.concatenate(blocks, axis=0).astype(BF16)
    kn = kn_s[pl.ds(r0, SUBLANES), :]
    kc = ck_ref[pl.ds(2 * c, 2)].reshape(2 * WINDOW, KV_W).astype(BF16)
    s_c = jnp.where(valid_c, _dot_t(lhs, kc) - bias_c, NEG)
    s_n = jnp.where(
        valid_n, _dot_t(lhs, jnp.concatenate([kn, pad], axis=0).astype(BF16)) - bias_n, NEG)
    return s_c, s_n

  def softmax(s_c, s_n):
    m = jnp.maximum(
        jnp.maximum(jnp.max(s_c, axis=-1, keepdims=True), jnp.max(s_n, axis=-1, keepdims=True)),
        sink)
    e_c = jnp.exp(s_c - m)
    e_n = jnp.exp(s_n - m)
    denom = (jnp.sum(e_c, axis=-1, keepdims=True) + jnp.sum(e_n, axis=-1, keepdims=True)
             + jnp.exp(sink - m))
    return e_c.astype(BF16), e_n.astype(BF16), 1.0 / denom

  def weighted_values(c, e_c, e_n, inv):
    r0 = pl.multiple_of(first_row + c * SUBLANES, SUBLANES)
    vn = vn_s[pl.ds(r0, SUBLANES), :]
    vc = cv_ref[pl.ds(2 * c, 2)].reshape(2 * WINDOW, KV_W).astype(BF16)
    o = (_dot(e_c, vc) + _dot(e_n, jnp.concatenate([vn, pad], axis=0).astype(BF16))) * inv
    for hp in range(N_PAIRS):
      lo = o[2 * hp * SUBLANES:(2 * hp + 1) * SUBLANES, :]
      hi = o[(2 * hp + 1) * SUBLANES:(2 * hp + 2) * SUBLANES, :]
      if (2 * hp) // GROUP == 0:
        hi = pltpu.roll(hi, HEAD_DIM, axis=1)
      else:
        lo = pltpu.roll(lo, HEAD_DIM, axis=1)
      attn_s[pl.ds(r0, SUBLANES), hp * KV_W:(hp + 1) * KV_W] = jnp.where(low_half, lo, hi)

  def shift_caches(c):
    r0 = pl.multiple_of(first_row + c * SUBLANES, SUBLANES)
    for new_ref, cache_ref, out_ref in ((kn_s, ck_ref, nk_ref), (vn_s, cv_ref, nv_ref)):
      out_ref[pl.ds(2 * c, 2), 0:kept, :] = cache_ref[pl.ds(2 * c, 2), n_tok:WINDOW, :]
      out_ref[pl.ds(2 * c, 1), kept:WINDOW, :] = new_ref[pl.ds(r0, n_tok), :].reshape(1, n_tok, KV_W)
      out_ref[pl.ds(2 * c + 1, 1), kept:WINDOW, :] = (
          new_ref[pl.ds(r0 + n_tok, n_tok), :].reshape(1, n_tok, KV_W))

  def pair_group(gi, carry):
    cs = [gi * SAMPLE_PAIRS_PER_ITER + i for i in range(SAMPLE_PAIRS_PER_ITER)]
    raw = [scores(c) for c in cs]
    probs = [softmax(*s) for s in raw]
    for c, pr in zip(cs, probs):
      weighted_values(c, *pr)
      shift_caches(c)
    return carry

  lax.fori_loop(0, SAMPLE_SEQS // 2 // SAMPLE_PAIRS_PER_ITER, pair_group, 0)

  @pl.when(step == pl.num_programs(0) - 1)
  def _():
    x = x_ref[...]
    xb = xb_ref[...]
    tok = lax.broadcasted_iota(jnp.int32, (rows, LRU_W), 0) % n_tok
    cpad = cpad_ref[...]
    xc = convb_ref[...] + convw_ref[CONV_W - 1:CONV_W, :] * xb
    for shift in range(1, CONV_W):
      saved = cpad if shift == CONV_W - 1 else pltpu.roll(cpad, rows - (CONV_W - 1 - shift), axis=0)
      prev = jnp.where(tok >= shift, pltpu.roll(xb, shift, axis=0), saved)
      xc = xc + convw_ref[CONV_W - 1 - shift:CONV_W - shift, :] * prev

    a, b = _lru_coeffs(xc, wa_ref, ba_ref, wi_ref, bi_ref, lam_ref)
    a, b = _scan_within_groups(a, b, n_tok)
    hs = a * h0_ref[...] + b
    hs_ref[...] = hs
    lru = hs * jax.nn.gelu(yb_s[...])

    hres = (x + _dot(attn_s[...].astype(BF16), wout_ref[0:ATTN_W, :])
            + _dot(lru.astype(BF16), wout_ref[ATTN_W:ATTN_W + LRU_W, :]) + bout_ref[...])
    hn = _rmsnorm(hres, gffn_ref[...]).astype(BF16)

    def write_output(first_row, out_rows):
      y_ref[first_row:first_row + out_rows.shape[0], :] = out_rows

    _Ffn(rows, lambda r0, r1: hres[r0:r1, :], lambda: hn, wfi_ref, wfo_ref, gfin_ref, act_s,
         write_output).finish()


def _sample_call(x, ck, cv, cpad, h0rep, sink_rows, weights):
  n_rows = x.shape[0]
  n_seqs = ck.shape[0]
  vm = _whole(pltpu.VMEM)
  cache_spec = pl.BlockSpec((SAMPLE_SEQS, WINDOW, KV_W), lambda i: (i, 0, 0))
  in_specs = [vm, vm, cache_spec, cache_spec, vm, vm] + [vm] * len(weights)
  out_shape = (
      jax.ShapeDtypeStruct((n_rows, D_MODEL), F32),
      jax.ShapeDtypeStruct((n_seqs, WINDOW, KV_W), F32),
      jax.ShapeDtypeStruct((n_seqs, WINDOW, KV_W), F32),
      jax.ShapeDtypeStruct((n_rows, LRU_W), F32),
      jax.ShapeDtypeStruct((n_rows, LRU_W), F32),
  )
  out_specs = (vm, cache_spec, cache_spec, vm, vm)
  scratch = [
      pltpu.VMEM((n_rows, ATTN_W), F32),
      pltpu.VMEM((n_rows, KV_W), F32),
      pltpu.VMEM((n_rows, KV_W), F32),
      pltpu.VMEM((n_rows, LRU_W), F32),
      pltpu.VMEM((n_rows, ATTN_W), F32),
      pltpu.VMEM((n_rows, D_FF), BF16),
  ]
  return pl.pallas_call(
      _sample_kernel,
      grid=(n_seqs // SAMPLE_SEQS,),
      in_specs=in_specs,
      out_specs=out_specs,
      out_shape=out_shape,
      scratch_shapes=scratch,
      compiler_params=pltpu.CompilerParams(
          dimension_semantics=("arbitrary",),
          vmem_limit_bytes=VMEM_LIMIT_BYTES),
      name="sample_layer",
  )(sink_rows, x, ck, cv, cpad, h0rep, *weights)


def _block_diag_halves(w):
  per = LRU_BLOCKS // 2
  blocks = w.reshape(2, per, 1, LRU_BW, LRU_BW)
  on_diag = jnp.eye(per, dtype=bool).reshape(1, per, per, 1, 1)
  tiles = jnp.where(on_diag, blocks, 0.0)
  return tiles.transpose(0, 1, 3, 2, 4).reshape(2, per * LRU_BW, per * LRU_BW).astype(BF16)


def kernel(x_prompt, x_sample, cache_k, cache_v, state_conv, state_h, g_mix, w_in, b_in,
           attn_sinks, conv_w, conv_b, w_a, b_a, w_i, b_i, lam, w_out, b_out, g_ffn,
           w_ffn_in, w_ffn_out, g_final):
  depth = g_mix.shape[0]
  assert depth == 1
  batch, seq, _ = x_prompt.shape
  dec_batch, dec_seq, _ = x_sample.shape
  assert seq % PROMPT_TILE == 0 and dec_seq == CONV_W and dec_batch % SAMPLE_SEQS == 0

  row = lambda p: p.reshape(1, -1)
  weights = [row(g_mix[0]), w_in[0].astype(BF16), row(b_in[0]), conv_w[0], row(conv_b[0]),
             _block_diag_halves(w_a[0]), row(b_a[0]), _block_diag_halves(w_i[0]), row(b_i[0]),
             row(lam[0]), w_out[0].astype(BF16), row(b_out[0]), row(g_ffn[0]),
             w_ffn_in[0].astype(BF16), w_ffn_out[0].astype(BF16), row(g_final)]

  y_p, nk_p, nv_p, nconv_p, nh_p = _prompt_call(x_prompt, attn_sinks[0], weights)

  n_rows = dec_batch * dec_seq
  cpad = jnp.pad(state_conv[0], ((0, 0), (0, 1), (0, 0))).reshape(n_rows, LRU_W)
  h0rep = jnp.repeat(state_h[0], dec_seq, axis=0)
  sink_rows = jnp.repeat(attn_sinks[0], SUBLANES).reshape(PAIR_ROWS, 1)
  y_s, nk_s, nv_s, xb_s, hs_s = _sample_call(
      x_sample.reshape(n_rows, D_MODEL), cache_k[0].reshape(dec_batch, WINDOW, KV_W),
      cache_v[0].reshape(dec_batch, WINDOW, KV_W), cpad, h0rep, sink_rows, weights)
  nconv_s = xb_s.reshape(dec_batch, dec_seq, LRU_W)[:, dec_seq - (CONV_W - 1):]
  nh_s = hs_s.reshape(dec_batch, dec_seq, LRU_W)[:, dec_seq - 1]

  pkv = (1, batch, WINDOW, N_KV_HEADS, HEAD_DIM)
  skv = (1, dec_batch, WINDOW, N_KV_HEADS, HEAD_DIM)
  return (y_p, y_s.reshape(dec_batch, dec_seq, D_MODEL),
          nk_p.reshape(pkv), nv_p.reshape(pkv), nconv_p[None], nh_p.reshape(1, batch, LRU_W),
          nk_s.reshape(skv), nv_s.reshape(skv), nconv_s[None], nh_s[None])
```

```python
import functools

import jax
import jax.numpy as jnp
from jax import lax
from jax.experimental import pallas as pl
from jax.experimental.pallas import tpu as pltpu

D_MODEL = 1024
N_HEADS = 8
N_KV_HEADS = 2
HEAD_DIM = 64
GROUP = N_HEADS // N_KV_HEADS
ATTN_W = N_HEADS * HEAD_DIM
KV_W = N_KV_HEADS * HEAD_DIM
WINDOW = 128
LRU_W = D_MODEL // 2
LRU_BLOCKS = 8
LRU_BW = LRU_W // LRU_BLOCKS
CONV_W = 4
LRU_C = 8.0
IN_W = ATTN_W + 2 * KV_W + 2 * LRU_W
Q_END = ATTN_W
K_END = Q_END + KV_W
V_END = K_END + KV_W
X_END = V_END + LRU_W
D_FF = 2816
EPS = 1e-6
NEG = -1e30
LOG2E = 1.4426950408889634

SUBLANES = 8
LANES = 128
VMEM_LIMIT_BYTES = 56 * 1024 * 1024

PROMPT_TILE = 512
FFN_CHUNK = 256
DOWN_ROWS = 256
LRU_ROWS = 64
N_PAIRS = N_HEADS // 2
SAMPLE_SEQS = 16
PAIR_ROWS = N_HEADS * SUBLANES
SAMPLE_PAIRS_PER_ITER = 8

F32 = jnp.float32
BF16 = jnp.bfloat16


def _dot(a, b):
  return jnp.dot(a, b, preferred_element_type=F32)


def _dot_t(a, b):
  return lax.dot_general(a, b, (((1,), (1,)), ((), ())), preferred_element_type=F32)


def _rmsnorm(x, g):
  return x * lax.rsqrt(jnp.mean(x * x, axis=-1, keepdims=True) + EPS) * g


def _alibi_slope(h):
  return 2.0 ** (-8.0 * (h + 1) / N_HEADS)


def _lru_coeffs(xc, wa_ref, ba_ref, wi_ref, bi_ref, lam_ref):
  xcb = xc.astype(BF16)
  half = LRU_W // 2
  lo, hi = xcb[:, :half], xcb[:, half:]
  r = jnp.concatenate([_dot(lo, wa_ref[0]), _dot(hi, wa_ref[1])], axis=-1) + ba_ref[...]
  ig = jnp.concatenate([_dot(lo, wi_ref[0]), _dot(hi, wi_ref[1])], axis=-1) + bi_ref[...]
  r = jax.nn.sigmoid(r)
  ig = jax.nn.sigmoid(ig)
  nlam = -lam_ref[...]
  softplus = jnp.maximum(nlam, 0.0) + jnp.log1p(jnp.exp(-jnp.abs(nlam)))
  log_a = r * (-LRU_C * softplus)
  a = jnp.exp(log_a)
  gain_sq = -jnp.tanh(log_a) * (a * a + 1.0)
  gain = jnp.where(gain_sq > 0.0, gain_sq * lax.rsqrt(gain_sq), 0.0)
  return a, gain * (ig * xc)


def _scan_within_groups(a, b, period):
  t, w = a.shape
  a3 = a.reshape(t // SUBLANES, SUBLANES, w)
  b3 = b.reshape(t // SUBLANES, SUBLANES, w)
  pos = lax.broadcasted_iota(jnp.int32, (1, SUBLANES, w), 1) % period
  d = 1
  while d < period:
    keep = pos >= d
    a_prev = jnp.where(keep, pltpu.roll(a3, d, axis=1), 1.0)
    b_prev = jnp.where(keep, pltpu.roll(b3, d, axis=1), 0.0)
    b3 = a3 * b_prev + b3
    a3 = a3 * a_prev
    d *= 2
  return a3.reshape(t, w), b3.reshape(t, w)


class _Ffn:
  def __init__(self, hres, hn, wfi_ref, wfo_ref, gfin_ref, act_ref, write):
    self.hres, self.hn, self.write = hres, hn, write
    self.wfi_ref, self.wfo_ref, self.gfin_ref, self.act_ref = wfi_ref, wfo_ref, gfin_ref, act_ref
    self.n_up = D_FF // FFN_CHUNK
    self.down_rows = min(DOWN_ROWS, hres.shape[0])
    self.n_units = self.n_up + hres.shape[0] // self.down_rows
    self.done = 0

  def run(self, n_units=1):
    for _ in range(min(n_units, self.n_units - self.done)):
      if self.done < self.n_up:
        lo = self.done * FFN_CHUNK
        w = jnp.concatenate([self.wfi_ref[:, lo:lo + FFN_CHUNK],
                             self.wfi_ref[:, D_FF + lo:D_FF + lo + FFN_CHUNK]], axis=-1)
        gate_up = _dot(self.hn, w)
        gate, up = gate_up[:, :FFN_CHUNK], gate_up[:, FFN_CHUNK:]
        self.act_ref[:, lo:lo + FFN_CHUNK] = (jax.nn.silu(gate) * up).astype(BF16)
      else:
        r0 = (self.done - self.n_up) * self.down_rows
        r1 = r0 + self.down_rows
        out = self.hres[r0:r1, :] + _dot(self.act_ref[r0:r1, :], self.wfo_ref[...])
        self.write(r0, _rmsnorm(out, self.gfin_ref[...]))
      self.done += 1

  def finish(self):
    self.run(self.n_units)


def _prompt_kernel(
    sinks_ref, x_ref, gmix_ref, win_ref, bin_ref, convw_ref, convb_ref, wa_ref, ba_ref,
    wi_ref, bi_ref, lam_ref, wout_ref, bout_ref, gffn_ref, wfi_ref, wfo_ref, gfin_ref,
    y_ref, nk_ref, nv_ref, nconv_ref, nh_ref,
    q_s, kz_s, vz_s, xpad_s, yb_s, hin_s, hcar_s, attn_s, lru_s, hres_s, hn_s, act_s, bias_s, *,
    tiles_per_seq):
  tq = PROMPT_TILE
  step = pl.program_id(0)
  slot = step % 2
  t_idx = step % tiles_per_seq
  first_tile = t_idx == 0
  last_tile = t_idx == tiles_per_seq - 1

  row = lax.broadcasted_iota(jnp.int32, (WINDOW, WINDOW), 0)
  col = lax.broadcasted_iota(jnp.int32, (WINDOW, WINDOW), 1)
  causal = col <= row
  dist = jnp.where(causal, row - col, WINDOW + row - col)

  @pl.when(step == 0)
  def _():
    hres_s[...] = jnp.zeros((2, tq, D_MODEL), F32)
    hn_s[...] = jnp.zeros((tq, D_MODEL), BF16)
    for h in range(N_HEADS):
      bias_s[h] = (_alibi_slope(h) * LOG2E) * dist.astype(F32)

  @pl.when(first_tile)
  def _():
    kz_s[:, 0:WINDOW, :] = jnp.zeros((4, WINDOW, KV_W), BF16)
    vz_s[:, 0:WINDOW, :] = jnp.zeros((4, WINDOW, KV_W), BF16)
    xpad_s[:, 0:SUBLANES, :] = jnp.zeros((LRU_W // LANES, SUBLANES, LANES), F32)
    hcar_s[...] = jnp.zeros((SUBLANES, LRU_W), F32)

  def write_output(first_row, rows):
    y_ref[0, first_row:first_row + rows.shape[0], :] = rows

  ffn = _Ffn(hres_s[1 - slot], hn_s[...], wfi_ref, wfo_ref, gfin_ref, act_s, write_output)
  ffn.run(1)

  x = x_ref[0]
  u = _rmsnorm(x, gmix_ref[...]).astype(BF16)

  q = _dot(u, win_ref[:, 0:Q_END]) + bin_ref[:, 0:Q_END]
  q_s[...] = (q * (HEAD_DIM ** -0.5 * LOG2E)).astype(BF16)
  w_kv = win_ref[:, Q_END:V_END]
  kv = jnp.concatenate([_dot(u[:tq // 2], w_kv), _dot(u[tq // 2:], w_kv)], axis=0)
  kv = kv + bin_ref[:, Q_END:V_END]
  k = kv[:, 0:KV_W]
  v = kv[:, KV_W:2 * KV_W]
  lru_in = {}

  def project_lru_section(name, lo):
    lru_in[name] = _dot(u, win_ref[:, lo:lo + LRU_W]) + bin_ref[:, lo:lo + LRU_W]

  half = tq // 2
  out_partial = {}

  def project_attention(rows):
    out_partial[rows] = _dot(attn_s[rows * half:(rows + 1) * half, :], wout_ref[0:ATTN_W, :])

  def finish_mixer(rows):
    r0 = rows * half
    lru = jnp.concatenate([lru_s[g, r0:r0 + half, :] for g in range(LRU_W // LANES)], axis=-1)
    hres = (x[r0:r0 + half, :] + out_partial[rows]
            + _dot(lru.astype(BF16), wout_ref[ATTN_W:ATTN_W + LRU_W, :]) + bout_ref[...])
    hres_s[slot, r0:r0 + half, :] = hres
    hn_s[r0:r0 + half, :] = _rmsnorm(hres, gffn_ref[...]).astype(BF16)

  n_problems = (tq // WINDOW) * N_PAIRS
  fixed_fillers = {0: functools.partial(project_lru_section, "xb", V_END),
                   1: functools.partial(project_lru_section, "yb", X_END),
                   n_problems - 3: functools.partial(project_attention, 0),
                   n_problems - 2: functools.partial(finish_mixer, 0)}

  def filler(i):
    if i in fixed_fillers:
      fixed_fillers[i]()
    else:
      ffn.run(1)

  lane = lax.broadcasted_iota(jnp.int32, (tq, KV_W), 1)
  is_lo = lane < HEAD_DIM
  for src, dst in ((k, kz_s), (v, vz_s)):
    swapped = pltpu.roll(src, HEAD_DIM, axis=1)
    dst[0, WINDOW:WINDOW + tq, :] = jnp.where(is_lo, src, 0.0).astype(BF16)
    dst[1, WINDOW:WINDOW + tq, :] = jnp.where(is_lo, 0.0, swapped).astype(BF16)
    dst[2, WINDOW:WINDOW + tq, :] = jnp.where(is_lo, swapped, 0.0).astype(BF16)
    dst[3, WINDOW:WINDOW + tq, :] = jnp.where(is_lo, 0.0, src).astype(BF16)

  lane_q = lax.broadcasted_iota(jnp.int32, (WINDOW, KV_W), 1)

  def operand(ref, j, p):
    g = p // (N_PAIRS // N_KV_HEADS)
    r0 = j * WINDOW
    return jnp.concatenate(
        [ref[2 * g, r0:r0 + 2 * WINDOW, :], ref[2 * g + 1, r0:r0 + 2 * WINDOW, :]], axis=0)

  def scores(j, p):
    r0 = j * WINDOW
    return _dot_t(q_s[r0:r0 + WINDOW, p * KV_W:(p + 1) * KV_W], operand(kz_s, j, p))

  def softmax(s, j, p):
    probs, inv = [], []
    for hh in range(2):
      h = 2 * p + hh
      sink = sinks_ref[h] * LOG2E
      c0 = hh * 2 * WINDOW
      sh = jnp.where(causal, s[:, c0 + WINDOW:c0 + 2 * WINDOW], s[:, c0:c0 + WINDOW]) - bias_s[h]
      if j == 0:
        sh = jnp.where(jnp.where(causal, -1, 0) < jnp.where(first_tile, 0, 1), sh, NEG)
      m = jnp.maximum(jnp.max(sh, axis=-1, keepdims=True), sink)
      e = jnp.exp2(sh - m)
      denom = jnp.sum(e, axis=-1, keepdims=True) + jnp.exp2(sink - m)
      e = e.astype(BF16)
      zero = jnp.zeros_like(e)
      probs += [jnp.where(causal, zero, e), jnp.where(causal, e, zero)]
      inv.append(1.0 / denom)
    return jnp.concatenate(probs, axis=-1), jnp.where(lane_q < HEAD_DIM, inv[0], inv[1])

  def weighted_values(j, p, probs, inv):
    r0 = j * WINDOW
    o = _dot(probs, operand(vz_s, j, p)) * inv
    attn_s[r0:r0 + WINDOW, p * KV_W:(p + 1) * KV_W] = o.astype(BF16)

  lru_state = {}
  n_groups = LRU_ROWS // SUBLANES

  lane_groups = LRU_W // LANES

  def put_rows(ref, first_row, val):
    for g in range(lane_groups):
      ref[g, first_row:first_row + val.shape[0], :] = val[:, g * LANES:(g + 1) * LANES]

  def phase_view(ref, first_row):
    return jnp.concatenate(
        [ref[g, pl.ds(first_row, n_groups, stride=SUBLANES), :] for g in range(lane_groups)], axis=-1)

  def phase_store(ref, first_row, val):
    for g in range(lane_groups):
      ref[g, pl.ds(first_row, n_groups, stride=SUBLANES), :] = val[:, g * LANES:(g + 1) * LANES]

  def lru_conv(rc):
    if rc == 0:
      put_rows(xpad_s, SUBLANES, lru_in["xb"])
      lru_state["h"] = hcar_s[0:1, :]
    c0 = SUBLANES + rc * LRU_ROWS
    shifted = {d: phase_view(xpad_s, c0 + d) for d in range(1 - CONV_W, SUBLANES)}
    phases = []
    for r in range(SUBLANES):
      xc = convb_ref[...] + convw_ref[CONV_W - 1:CONV_W, :] * shifted[r]
      for kk in range(CONV_W - 1):
        xc = xc + convw_ref[kk:kk + 1, :] * shifted[r - (CONV_W - 1) + kk]
      phases.append(xc)
    lru_state["xc"] = jnp.concatenate(phases, axis=0)

  def lru_recurrence(rc):
    if rc == 0:
      put_rows(yb_s, 0, lru_in["yb"])
    c0 = rc * LRU_ROWS
    a, b = _lru_coeffs(lru_state["xc"], wa_ref, ba_ref, wi_ref, bi_ref, lam_ref)
    phase = lambda t, r: t[r * n_groups:(r + 1) * n_groups, :]
    a_cum, b_cum = [phase(a, 0)], [phase(b, 0)]
    for r in range(1, SUBLANES):
      b_cum.append(phase(a, r) * b_cum[-1] + phase(b, r))
      a_cum.append(phase(a, r) * a_cum[-1])
    h = lru_state["h"]
    for gi in range(n_groups):
      hin_s[gi:gi + 1, :] = h
      h = a_cum[-1][gi:gi + 1, :] * h + b_cum[-1][gi:gi + 1, :]
    lru_state["h"] = h
    h_in = hin_s[...]
    for r in range(SUBLANES):
      hs = a_cum[r] * h_in + b_cum[r]
      gate = jax.nn.gelu(phase_view(yb_s, c0 + r))
      phase_store(lru_s, c0 + r, hs * gate)

  problems = [(j, p) for j in range(tq // WINDOW) for p in range(N_PAIRS)]
  lru_period = len(problems) // (tq // LRU_ROWS)
  raw = {0: scores(*problems[0])}
  ready = {}
  for i, (j, p) in enumerate(problems):
    if i + 1 < len(problems):
      raw[i + 1] = scores(*problems[i + 1])
    ready[i] = softmax(raw.pop(i), j, p)
    filler(i)
    if i % lru_period == 1:
      lru_recurrence(i // lru_period)
    if i >= 1:
      weighted_values(*problems[i - 1], *ready.pop(i - 1))
    if i % lru_period == 0:
      lru_conv(i // lru_period)
  weighted_values(*problems[-1], *ready.pop(len(problems) - 1))

  kz_s[:, 0:WINDOW, :] = kz_s[:, tq:tq + WINDOW, :]
  vz_s[:, 0:WINDOW, :] = vz_s[:, tq:tq + WINDOW, :]
  xb = lru_in["xb"]
  h = lru_state["h"]
  xpad_s[:, 0:SUBLANES, :] = xpad_s[:, tq:tq + SUBLANES, :]
  hcar_s[...] = jnp.broadcast_to(h, (SUBLANES, LRU_W))

  @pl.when(last_tile)
  def _():
    nk_ref[0] = k[tq - WINDOW:tq, :]
    nv_ref[0] = v[tq - WINDOW:tq, :]
    nconv_ref[0] = xb[tq - (CONV_W - 1):tq, :]
    nh_ref[0] = h

  project_attention(1)
  finish_mixer(1)
  ffn.finish()


def _whole(space):
  return pl.BlockSpec(memory_space=space)


def _prompt_call(x, sinks, weights):
  batch, seq, _ = x.shape
  tq = PROMPT_TILE
  n_t = seq // tq
  n_tiles = batch * n_t
  vm = _whole(pltpu.VMEM)

  def mixer_tile(s):
    return jnp.minimum(s, n_tiles - 1)

  def ffn_tile(s):
    return jnp.maximum(s - 1, 0)

  in_specs = [_whole(pltpu.SMEM),
              pl.BlockSpec((1, tq, D_MODEL), lambda s: (mixer_tile(s) // n_t, mixer_tile(s) % n_t, 0))]
  in_specs += [vm] * len(weights)
  out_shape = (
      jax.ShapeDtypeStruct((batch, seq, D_MODEL), F32),
      jax.ShapeDtypeStruct((batch, WINDOW, KV_W), F32),
      jax.ShapeDtypeStruct((batch, WINDOW, KV_W), F32),
      jax.ShapeDtypeStruct((batch, CONV_W - 1, LRU_W), F32),
      jax.ShapeDtypeStruct((batch, 1, LRU_W), F32),
  )
  state_map = lambda s: (mixer_tile(s) // n_t, 0, 0)
  out_specs = (
      pl.BlockSpec((1, tq, D_MODEL), lambda s: (ffn_tile(s) // n_t, ffn_tile(s) % n_t, 0)),
      pl.BlockSpec((1, WINDOW, KV_W), state_map),
      pl.BlockSpec((1, WINDOW, KV_W), state_map),
      pl.BlockSpec((1, CONV_W - 1, LRU_W), state_map),
      pl.BlockSpec((1, 1, LRU_W), state_map),
  )
  scratch = [
      pltpu.VMEM((tq, ATTN_W), BF16),
      pltpu.VMEM((4, WINDOW + tq, KV_W), BF16),
      pltpu.VMEM((4, WINDOW + tq, KV_W), BF16),
      pltpu.VMEM((LRU_W // LANES, SUBLANES + tq, LANES), F32),
      pltpu.VMEM((LRU_W // LANES, tq, LANES), F32),
      pltpu.VMEM((LRU_ROWS // SUBLANES, LRU_W), F32),
      pltpu.VMEM((SUBLANES, LRU_W), F32),
      pltpu.VMEM((tq, ATTN_W), BF16),
      pltpu.VMEM((LRU_W // LANES, tq, LANES), F32),
      pltpu.VMEM((2, tq, D_MODEL), F32),
      pltpu.VMEM((tq, D_MODEL), BF16),
      pltpu.VMEM((tq, D_FF), BF16),
      pltpu.VMEM((N_HEADS, WINDOW, WINDOW), F32),
  ]
  return pl.pallas_call(
      functools.partial(_prompt_kernel, tiles_per_seq=n_t),
      grid=(n_tiles + 1,),
      in_specs=in_specs,
      out_specs=out_specs,
      out_shape=out_shape,
      scratch_shapes=scratch,
      compiler_params=pltpu.CompilerParams(
          dimension_semantics=("arbitrary",),
          vmem_limit_bytes=VMEM_LIMIT_BYTES),
      name="prompt_layer",
  )(sinks, x, *weights)


def _sample_kernel(
    sink_ref, x_ref, ck_ref, cv_ref, cpad_ref, h0_ref, gmix_ref, win_ref, bin_ref, convw_ref,
    convb_ref, wa_ref, ba_ref, wi_ref, bi_ref, lam_ref, wout_ref, bout_ref, gffn_ref, wfi_ref,
    wfo_ref, gfin_ref,
    y_ref, nk_ref, nv_ref, xb_ref, hs_ref,
    q_s, kn_s, vn_s, yb_s, attn_s, act_s):
  n_tok = CONV_W
  rows = x_ref.shape[0]
  step = pl.program_id(0)

  @pl.when(step == 0)
  def _():
    u = _rmsnorm(x_ref[...], gmix_ref[...]).astype(BF16)
    q_s[...] = (_dot(u, win_ref[:, 0:Q_END]) + bin_ref[:, 0:Q_END]) * (HEAD_DIM ** -0.5)
    kv = _dot(u, win_ref[:, Q_END:V_END]) + bin_ref[:, Q_END:V_END]
    kn_s[...] = kv[:, 0:KV_W]
    vn_s[...] = kv[:, KV_W:2 * KV_W]
    xb_ref[...] = _dot(u, win_ref[:, V_END:X_END]) + bin_ref[:, V_END:X_END]
    yb_s[...] = _dot(u, win_ref[:, X_END:IN_W]) + bin_ref[:, X_END:IN_W]

  prow = lax.broadcasted_iota(jnp.int32, (PAIR_ROWS, 2 * WINDOW), 0)
  pcol = lax.broadcasted_iota(jnp.int32, (PAIR_ROWS, 2 * WINDOW), 1)
  seq_r = (prow % SUBLANES) // n_tok
  tok_r = prow % n_tok
  dist_c = WINDOW + tok_r - (pcol % WINDOW)
  valid_c = (seq_r == pcol // WINDOW) & (dist_c >= 0) & (dist_c < WINDOW)
  nrow = lax.broadcasted_iota(jnp.int32, (PAIR_ROWS, LANES), 0)
  ncol = lax.broadcasted_iota(jnp.int32, (PAIR_ROWS, LANES), 1)
  dist_n = (nrow % n_tok) - (ncol % n_tok)
  valid_n = (ncol < SUBLANES) & ((nrow % SUBLANES) // n_tok == ncol // n_tok) & (dist_n >= 0)
  slope_c = jnp.zeros((PAIR_ROWS, 2 * WINDOW), F32)
  slope_n = jnp.zeros((PAIR_ROWS, LANES), F32)
  for h in range(N_HEADS):
    slope_c = jnp.where(prow // SUBLANES == h, _alibi_slope(h), slope_c)
    slope_n = jnp.where(nrow // SUBLANES == h, _alibi_slope(h), slope_n)
  bias_c = slope_c * dist_c.astype(F32)
  bias_n = slope_n * dist_n.astype(F32)
  sink = sink_ref[...]
  pad = jnp.zeros((LANES - SUBLANES, KV_W), F32)
  low_half = lax.broadcasted_iota(jnp.int32, (SUBLANES, KV_W), 1) < HEAD_DIM
  first_row = step * (SAMPLE_SEQS * n_tok)
  kept = WINDOW - n_tok

  def scores(c):
    r0 = pl.multiple_of(first_row + c * SUBLANES, SUBLANES)
    blocks = []
    for h in range(N_HEADS):
      grp = q_s[pl.ds(r0, SUBLANES), (h // 2) * KV_W:(h // 2 + 1) * KV_W]
      to_low = h // GROUP == 0
      if (h % 2 == 0) != to_low:
        grp = pltpu.roll(grp, HEAD_DIM, axis=1)
      blocks.append(jnp.where(low_half == to_low, grp, 0.0))
    lhs = jnp.concatenate(blocks, axis=0).astype(BF16)
    kn = kn_s[pl.ds(r0, SUBLANES), :]
    kc = ck_ref[pl.ds(2 * c, 2)].reshape(2 * WINDOW, KV_W).astype(BF16)
    s_c = jnp.where(valid_c, _dot_t(lhs, kc) - bias_c, NEG)
    s_n = jnp.where(
        valid_n, _dot_t(lhs, jnp.concatenate([kn, pad], axis=0).astype(BF16)) - bias_n, NEG)
    return s_c, s_n

  def softmax(s_c, s_n):
    m = jnp.maximum(
        jnp.maximum(jnp.max(s_c, axis=-1, keepdims=True), jnp.max(s_n, axis=-1, keepdims=True)),
        sink)
    e_c = jnp.exp(s_c - m)
    e_n = jnp.exp(s_n - m)
    denom = (jnp.sum(e_c, axis=-1, keepdims=True) + jnp.sum(e_n, axis=-1, keepdims=True)
             + jnp.exp(sink - m))
    return e_c.astype(BF16), e_n.astype(BF16), 1.0 / denom

  def weighted_values(c, e_c, e_n, inv):
    r0 = pl.multiple_of(first_row + c * SUBLANES, SUBLANES)
    vn = vn_s[pl.ds(r0, SUBLANES), :]
    vc = cv_ref[pl.ds(2 * c, 2)].reshape(2 * WINDOW, KV_W).astype(BF16)
    o = (_dot(e_c, vc) + _dot(e_n, jnp.concatenate([vn, pad], axis=0).astype(BF16))) * inv
    for hp in range(N_PAIRS):
      lo = o[2 * hp * SUBLANES:(2 * hp + 1) * SUBLANES, :]
      hi = o[(2 * hp + 1) * SUBLANES:(2 * hp + 2) * SUBLANES, :]
      if (2 * hp) // GROUP == 0:
        hi = pltpu.roll(hi, HEAD_DIM, axis=1)
      else:
        lo = pltpu.roll(lo, HEAD_DIM, axis=1)
      attn_s[pl.ds(r0, SUBLANES), hp * KV_W:(hp + 1) * KV_W] = jnp.where(low_half, lo, hi)

  def shift_caches(c):
    r0 = pl.multiple_of(first_row + c * SUBLANES, SUBLANES)
    for new_ref, cache_ref, out_ref in ((kn_s, ck_ref, nk_ref), (vn_s, cv_ref, nv_ref)):
      out_ref[pl.ds(2 * c, 2), 0:kept, :] = cache_ref[pl.ds(2 * c, 2), n_tok:WINDOW, :]
      out_ref[pl.ds(2 * c, 1), kept:WINDOW, :] = new_ref[pl.ds(r0, n_tok), :].reshape(1, n_tok, KV_W)
      out_ref[pl.ds(2 * c + 1, 1), kept:WINDOW, :] = (
          new_ref[pl.ds(r0 + n_tok, n_tok), :].reshape(1, n_tok, KV_W))

  def pair_group(gi, carry):
    cs = [gi * SAMPLE_PAIRS_PER_ITER + i for i in range(SAMPLE_PAIRS_PER_ITER)]
    raw = [scores(c) for c in cs]
    probs = [softmax(*s) for s in raw]
    for c, pr in zip(cs, probs):
      weighted_values(c, *pr)
      shift_caches(c)
    return carry

  lax.fori_loop(0, SAMPLE_SEQS // 2 // SAMPLE_PAIRS_PER_ITER, pair_group, 0)

  @pl.when(step == pl.num_programs(0) - 1)
  def _():
    x = x_ref[...]
    xb = xb_ref[...]
    tok = lax.broadcasted_iota(jnp.int32, (rows, LRU_W), 0) % n_tok
    cpad = cpad_ref[...]
    xc = convb_ref[...] + convw_ref[CONV_W - 1:CONV_W, :] * xb
    for shift in range(1, CONV_W):
      saved = cpad if shift == CONV_W - 1 else pltpu.roll(cpad, rows - (CONV_W - 1 - shift), axis=0)
      prev = jnp.where(tok >= shift, pltpu.roll(xb, shift, axis=0), saved)
      xc = xc + convw_ref[CONV_W - 1 - shift:CONV_W - shift, :] * prev

    a, b = _lru_coeffs(xc, wa_ref, ba_ref, wi_ref, bi_ref, lam_ref)
    a, b = _scan_within_groups(a, b, n_tok)
    hs = a * h0_ref[...] + b
    hs_ref[...] = hs
    lru = hs * jax.nn.gelu(yb_s[...])

    hres = (x + _dot(attn_s[...].astype(BF16), wout_ref[0:ATTN_W, :])
            + _dot(lru.astype(BF16), wout_ref[ATTN_W:ATTN_W + LRU_W, :]) + bout_ref[...])
    hn = _rmsnorm(hres, gffn_ref[...]).astype(BF16)

    def write_output(first_row, out_rows):
      y_ref[first_row:first_row + out_rows.shape[0], :] = out_rows

    _Ffn(hres, hn, wfi_ref, wfo_ref, gfin_ref, act_s, write_output).finish()


def _sample_call(x, ck, cv, cpad, h0rep, sink_rows, weights):
  n_rows = x.shape[0]
  n_seqs = ck.shape[0]
  vm = _whole(pltpu.VMEM)
  cache_spec = pl.BlockSpec((SAMPLE_SEQS, WINDOW, KV_W), lambda i: (i, 0, 0))
  in_specs = [vm, vm, cache_spec, cache_spec, vm, vm] + [vm] * len(weights)
  out_shape = (
      jax.ShapeDtypeStruct((n_rows, D_MODEL), F32),
      jax.ShapeDtypeStruct((n_seqs, WINDOW, KV_W), F32),
      jax.ShapeDtypeStruct((n_seqs, WINDOW, KV_W), F32),
      jax.ShapeDtypeStruct((n_rows, LRU_W), F32),
      jax.ShapeDtypeStruct((n_rows, LRU_W), F32),
  )
  out_specs = (vm, cache_spec, cache_spec, vm, vm)
  scratch = [
      pltpu.VMEM((n_rows, ATTN_W), F32),
      pltpu.VMEM((n_rows, KV_W), F32),
      pltpu.VMEM((n_rows, KV_W), F32),
      pltpu.VMEM((n_rows, LRU_W), F32),
      pltpu.VMEM((n_rows, ATTN_W), F32),
      pltpu.VMEM((n_rows, D_FF), BF16),
  ]
  return pl.pallas_call(
      _sample_kernel,
      grid=(n_seqs // SAMPLE_SEQS,),
      in_specs=in_specs,
      out_specs=out_specs,
      out_shape=out_shape,
      scratch_shapes=scratch,
      compiler_params=pltpu.CompilerParams(
          dimension_semantics=("arbitrary",),
          vmem_limit_bytes=VMEM_LIMIT_BYTES),
      name="sample_layer",
  )(sink_rows, x, ck, cv, cpad, h0rep, *weights)


def _block_diag_halves(w):
  per = LRU_BLOCKS // 2
  blocks = w.reshape(2, per, 1, LRU_BW, LRU_BW)
  on_diag = jnp.eye(per, dtype=bool).reshape(1, per, per, 1, 1)
  tiles = jnp.where(on_diag, blocks, 0.0)
  return tiles.transpose(0, 1, 3, 2, 4).reshape(2, per * LRU_BW, per * LRU_BW).astype(BF16)


def kernel(x_prompt, x_sample, cache_k, cache_v, state_conv, state_h, g_mix, w_in, b_in,
           attn_sinks, conv_w, conv_b, w_a, b_a, w_i, b_i, lam, w_out, b_out, g_ffn,
           w_ffn_in, w_ffn_out, g_final):
  depth = g_mix.shape[0]
  assert depth == 1
  batch, seq, _ = x_prompt.shape
  dec_batch, dec_seq, _ = x_sample.shape
  assert seq % PROMPT_TILE == 0 and dec_seq == CONV_W and dec_batch % SAMPLE_SEQS == 0

  row = lambda p: p.reshape(1, -1)
  weights = [row(g_mix[0]), w_in[0].astype(BF16), row(b_in[0]), conv_w[0], row(conv_b[0]),
             _block_diag_halves(w_a[0]), row(b_a[0]), _block_diag_halves(w_i[0]), row(b_i[0]),
             row(lam[0]), w_out[0].astype(BF16), row(b_out[0]), row(g_ffn[0]),
             w_ffn_in[0].astype(BF16), w_ffn_out[0].astype(BF16), row(g_final)]

  y_p, nk_p, nv_p, nconv_p, nh_p = _prompt_call(x_prompt, attn_sinks[0], weights)

  n_rows = dec_batch * dec_seq
  cpad = jnp.pad(state_conv[0], ((0, 0), (0, 1), (0, 0))).reshape(n_rows, LRU_W)
  h0rep = jnp.repeat(state_h[0], dec_seq, axis=0)
  sink_rows = jnp.repeat(attn_sinks[0], SUBLANES).reshape(PAIR_ROWS, 1)
  y_s, nk_s, nv_s, xb_s, hs_s = _sample_call(
      x_sample.reshape(n_rows, D_MODEL), cache_k[0].reshape(dec_batch, WINDOW, KV_W),
      cache_v[0].reshape(dec_batch, WINDOW, KV_W), cpad, h0rep, sink_rows, weights)
  nconv_s = xb_s.reshape(dec_batch, dec_seq, LRU_W)[:, dec_seq - (CONV_W - 1):]
  nh_s = hs_s.reshape(dec_batch, dec_seq, LRU_W)[:, dec_seq - 1]

  pkv = (1, batch, WINDOW, N_KV_HEADS, HEAD_DIM)
  skv = (1, dec_batch, WINDOW, N_KV_HEADS, HEAD_DIM)
  return (y_p, y_s.reshape(dec_batch, dec_seq, D_MODEL),
          nk_p.reshape(pkv), nv_p.reshape(pkv), nconv_p[None], nh_p.reshape(1, batch, LRU_W),
          nk_s.reshape(skv), nv_s.reshape(skv), nconv_s[None], nh_s[None])
```
